```python
import math
import jax, jax.numpy as jnp
from jax import lax
import numpy as np

D_MODEL = 4096
BATCH = 2
SEQ = 8192
DEPTH = 1

SB_HEADS = 16
SB_HEAD_DIM = 128
SB_WIDTH = SB_HEADS * SB_HEAD_DIM
SB_Q_BLOCK = 128
MOBA_HEADS = 16
MOBA_HEAD_DIM = 128
MOBA_WIDTH = MOBA_HEADS * MOBA_HEAD_DIM
MOBA_BLOCK = 256
MOBA_TOPK = 3
MOBA_Q_CHUNK = 32
N_BRANCHES = 2
QKV_WIDTH = 3 * SB_WIDTH + 3 * MOBA_WIDTH
N_EXPERTS = 32
TOP_K = 4
D_FF_EXPERT = 1536
SWIGLU_ALPHA = 1.702
SWIGLU_LIMIT = 7.0
RMS_EPS = 1e-5

kernel_name = "hybrid_stickbreak_moba_moe_block"


def rmsnorm(x, g):
    xf = x.astype(jnp.float32)
    y = xf * lax.rsqrt(jnp.mean(xf * xf, axis=-1, keepdims=True) + RMS_EPS)
    return (y * g.astype(jnp.float32)).astype(x.dtype)


def to_heads(a, n_heads):
    b, t, _ = a.shape
    return a.reshape(b, t, n_heads, -1).transpose(0, 2, 1, 3)


def from_heads(a):
    b, h, t, d = a.shape
    return a.transpose(0, 2, 1, 3).reshape(b, t, h * d)


def alibi_slopes(n_heads):
    return jnp.exp2(-8.0 * jnp.arange(1, n_heads + 1, dtype=jnp.float32) / n_heads)


def stick_breaking_attention(q, k, v):
    b, h, t, d = q.shape
    n_blk = t // SB_Q_BLOCK
    scale = 1.0 / math.sqrt(d)
    kf = k.astype(jnp.float32)
    vf = v.astype(jnp.float32)
    key_pos = jnp.arange(t)
    q_blocks = q.reshape(b, h, n_blk, SB_Q_BLOCK, d).transpose(2, 0, 1, 3, 4)

    def one_block(args):
        i, qb = args
        z = jnp.einsum('bhqd,bhkd->bhqk', qb.astype(jnp.float32), kf) * scale
        q_pos = i * SB_Q_BLOCK + jnp.arange(SB_Q_BLOCK)
        past = key_pos[None, :] < q_pos[:, None]
        sp = jnp.where(past, jax.nn.softplus(z), 0.0)
        between = lax.cumsum(sp, axis=3, reverse=True) - sp
        w = jnp.where(past, jnp.exp(jax.nn.log_sigmoid(z) - between), 0.0)
        return jnp.einsum('bhqk,bhkd->bhqd', w, vf)

    out = lax.map(one_block, (jnp.arange(n_blk), q_blocks))
    return out.transpose(1, 2, 0, 3, 4).reshape(b, h, t, d).astype(q.dtype)


def moba_attention(q, k, v, slopes):
    b, h, t, d = q.shape
    scale = 1.0 / math.sqrt(d)
    n_blocks = -(-t // MOBA_BLOCK)
    t_pad = n_blocks * MOBA_BLOCK
    pad = ((0, 0), (0, 0), (0, t_pad - t), (0, 0))
    qp = jnp.pad(q.astype(jnp.float32), pad)
    kp = jnp.pad(k.astype(jnp.float32), pad)
    vp = jnp.pad(v.astype(jnp.float32), pad)
    k_blocks = kp.reshape(b, h, n_blocks, MOBA_BLOCK, d)
    v_blocks = vp.reshape(b, h, n_blocks, MOBA_BLOCK, d)
    k_mean = jnp.mean(k_blocks, axis=3)
    n_sel = min(MOBA_TOPK, n_blocks)
    n_chunks = t_pad // MOBA_Q_CHUNK
    q_chunks = qp.reshape(b, h, n_chunks, MOBA_Q_CHUNK, d).transpose(2, 0, 1, 3, 4)
    bi = jnp.arange(b)[:, None, None, None]
    hi = jnp.arange(h)[None, :, None, None]
    offs = jnp.arange(MOBA_BLOCK)
    block_ids = jnp.arange(n_blocks)
    sel_ids = jnp.arange(n_sel)

    def one_chunk(args):
        c, qc = args
        q_pos = c * MOBA_Q_CHUNK + jnp.arange(MOBA_Q_CHUNK)
        own = (c * MOBA_Q_CHUNK) // MOBA_BLOCK
        gate = jnp.einsum('bhqd,bhnd->bhqn', qc, k_mean)
        gate = jnp.where(block_ids < own, gate, -jnp.inf)
        _, sel = lax.top_k(gate, n_sel)
        valid = sel_ids < jnp.minimum(own, MOBA_TOPK)
        k_sel = k_blocks[bi, hi, sel]
        v_sel = v_blocks[bi, hi, sel]
        key_pos_sel = sel[..., None] * MOBA_BLOCK + offs
        s_sel = jnp.einsum('bhqd,bhqjkd->bhqjk', qc, k_sel) * scale
        s_sel = s_sel - slopes[:, None, None, None] * (q_pos[:, None, None] - key_pos_sel)
        s_sel = jnp.where(valid[:, None], s_sel, -jnp.inf)
        k_own = lax.dynamic_index_in_dim(k_blocks, own, axis=2, keepdims=False)
        v_own = lax.dynamic_index_in_dim(v_blocks, own, axis=2, keepdims=False)
        own_pos = own * MOBA_BLOCK + offs
        s_own = jnp.einsum('bhqd,bhkd->bhqk', qc, k_own) * scale
        s_own = s_own - slopes[:, None, None] * (q_pos[:, None] - own_pos[None, :])
        s_own = jnp.where(own_pos[None, :] <= q_pos[:, None], s_own, -jnp.inf)
        n_g = n_sel * MOBA_BLOCK
        logits = jnp.concatenate([s_sel.reshape(b, h, MOBA_Q_CHUNK, n_g), s_own], axis=-1)
        p = jax.nn.softmax(logits, axis=-1)
        p_sel = p[..., :n_g].reshape(b, h, MOBA_Q_CHUNK, n_sel, MOBA_BLOCK)
        p_own = p[..., n_g:]
        return (jnp.einsum('bhqjk,bhqjkd->bhqd', p_sel, v_sel)
                + jnp.einsum('bhqk,bhkd->bhqd', p_own, v_own))

    out = lax.map(one_chunk, (jnp.arange(n_chunks), q_chunks))
    out = out.transpose(1, 2, 0, 3, 4).reshape(b, h, t_pad, d)[:, :, :t]
    return out.astype(q.dtype)


def clamped_swiglu(g, u):
    g = jnp.minimum(g, SWIGLU_LIMIT)
    u = jnp.clip(u, -SWIGLU_LIMIT, SWIGLU_LIMIT)
    return g * jax.nn.sigmoid(SWIGLU_ALPHA * g) * (u + 1.0)


def moe_ffn(h, w_router, b_router, w_gate_up, b_gate_up, w_down, b_down):
    b, t, d = h.shape
    tok = h.reshape(b * t, d)
    logits = (tok @ w_router + b_router).astype(jnp.float32)
    top_vals, top_idx = lax.top_k(logits, TOP_K)
    top_w = jax.nn.softmax(top_vals, axis=-1)
    combine = jnp.einsum('nk,nke->ne', top_w, jax.nn.one_hot(top_idx, N_EXPERTS, dtype=jnp.float32))
    out = jnp.zeros((b * t, d), jnp.float32)
    for e in range(N_EXPERTS):
        gu = tok @ w_gate_up[e] + b_gate_up[e]
        act = clamped_swiglu(gu[:, :D_FF_EXPERT], gu[:, D_FF_EXPERT:])
        y = act @ w_down[e] + b_down[e]
        out = out + combine[:, e:e + 1] * y.astype(jnp.float32)
    return out.astype(h.dtype).reshape(b, t, d)


def setup_inputs(seed: int = 0) -> dict:
    key = jax.random.key(seed)
    ks = jax.random.split(key, 17)
    f32 = jnp.float32
    nrm = lambda k, shape, s: jax.random.normal(k, shape, f32) * s
    D, L, E, F = D_MODEL, DEPTH, N_EXPERTS, D_FF_EXPERT
    return {
        "x": jax.random.normal(ks[0], (BATCH, SEQ, D), f32),
        "norm_mix": 1.0 + nrm(ks[1], (L, D), 0.01),
        "w_in": nrm(ks[2], (L, D, QKV_WIDTH), D ** -0.5),
        "w_up_sb": nrm(ks[3], (L, SB_WIDTH, D), SB_WIDTH ** -0.5),
        "w_up_moba": nrm(ks[4], (L, MOBA_WIDTH, D), MOBA_WIDTH ** -0.5),
        "w_branch_gate": nrm(ks[5], (L, D, N_BRANCHES * D), D ** -0.5),
        "b_branch_gate": nrm(ks[6], (L, N_BRANCHES * D), 0.01),
        "w_out": nrm(ks[7], (L, D, D), D ** -0.5),
        "norm_ffn": 1.0 + nrm(ks[8], (L, D), 0.01),
        "w_router": nrm(ks[9], (L, D, E), D ** -0.5),
        "b_router": nrm(ks[10], (L, E), 0.01),
        "w_gate_up": nrm(ks[11], (L, E, D, 2 * F), D ** -0.5),
        "b_gate_up": nrm(ks[12], (L, E, 2 * F), 0.01),
        "w_down": nrm(ks[13], (L, E, F, D), F ** -0.5),
        "b_down": nrm(ks[14], (L, E, D), 0.01),
        "norm_final": 1.0 + nrm(ks[15], (D,), 0.01),
    }


def reference(x, norm_mix, w_in, w_up_sb, w_up_moba, w_branch_gate, b_branch_gate, w_out,
              norm_ffn, w_router, b_router, w_gate_up, b_gate_up, w_down, b_down, norm_final):
    b, t, _ = x.shape
    slopes = alibi_slopes(MOBA_HEADS)
    cuts = [SB_WIDTH, 2 * SB_WIDTH, 3 * SB_WIDTH,
            3 * SB_WIDTH + MOBA_WIDTH, 3 * SB_WIDTH + 2 * MOBA_WIDTH]
    for l in range(DEPTH):
        h = rmsnorm(x, norm_mix[l])
        proj = h @ w_in[l]
        q_sb, k_sb, v_sb, q_mb, k_mb, v_mb = jnp.split(proj, cuts, axis=-1)
        o_sb = stick_breaking_attention(to_heads(q_sb, SB_HEADS), to_heads(k_sb, SB_HEADS),
                                        to_heads(v_sb, SB_HEADS))
        o_mb = moba_attention(to_heads(q_mb, MOBA_HEADS), to_heads(k_mb, MOBA_HEADS),
                              to_heads(v_mb, MOBA_HEADS), slopes)
        u_sb = from_heads(o_sb) @ w_up_sb[l]
        u_mb = from_heads(o_mb) @ w_up_moba[l]
        gates = jax.nn.sigmoid(h @ w_branch_gate[l] + b_branch_gate[l]).reshape(
            b, t, N_BRANCHES, D_MODEL)
        mixed = gates[:, :, 0] * u_sb + gates[:, :, 1] * u_mb
        x = x + mixed @ w_out[l]
        h = rmsnorm(x, norm_ffn[l])
        x = x + moe_ffn(h, w_router[l], b_router[l], w_gate_up[l], b_gate_up[l],
                        w_down[l], b_down[l])
    return rmsnorm(x, norm_final)
```

```python
import functools
import math

import jax
import jax.numpy as jnp
from jax import lax
from jax.experimental import pallas as pl
from jax.experimental.pallas import tpu as pltpu

F32 = jnp.float32
BF16 = jnp.bfloat16

HEAD_DIM = 128
MOBA_BLOCK = 256
MOBA_TOPK = 3
TOP_K = 4
SWIGLU_ALPHA = 1.702
SWIGLU_LIMIT = 7.0
RMS_EPS = 1e-5
LANES = 128
NEG_BIG = -1e30
VMEM_LIMIT = 56 * 1024 * 1024


def _cparams(sem):
    return pltpu.CompilerParams(dimension_semantics=sem, vmem_limit_bytes=VMEM_LIMIT)


def _tile(dim, target, align=LANES):
    if dim <= target:
        return dim
    t = (target // align) * align
    while t > align and dim % t:
        t -= align
    assert dim % t == 0, (dim, target)
    return t


def _rms(x, g):
    ms = jnp.mean(x * x, axis=-1, keepdims=True)
    return x * lax.rsqrt(ms + RMS_EPS) * g


def _rmsnorm_kernel(x_ref, g_ref, o_ref):
    o_ref[...] = _rms(x_ref[...].astype(F32), g_ref[...]).astype(o_ref.dtype)


def rmsnorm(x, g, out_dtype, tm=256):
    n, d = x.shape
    tm = _tile(n, tm, 8)
    return pl.pallas_call(
        _rmsnorm_kernel,
        out_shape=jax.ShapeDtypeStruct((n, d), out_dtype),
        grid=(n // tm,),
        in_specs=[pl.BlockSpec((tm, d), lambda i: (i, 0)),
                  pl.BlockSpec((1, d), lambda i: (0, 0))],
        out_specs=pl.BlockSpec((tm, d), lambda i: (i, 0)),
        compiler_params=_cparams(("parallel",)),
        name="rmsnorm",
    )(x, g.reshape(1, d).astype(F32))


def _mm_kernel(*refs, epilogue, nk):
    if epilogue == "none":
        a_ref, b_ref, o_ref, acc_ref = refs
        e_ref = None
    else:
        a_ref, b_ref, e_ref, o_ref, acc_ref = refs
    k = pl.program_id(2)

    @pl.when(k == 0)
    def _():
        acc_ref[...] = jnp.zeros_like(acc_ref)

    acc_ref[...] += jnp.dot(a_ref[...], b_ref[...], preferred_element_type=F32)

    @pl.when(k == nk - 1)
    def _():
        acc = acc_ref[...]
        if epilogue == "sigmoid_bias":
            acc = jax.nn.sigmoid(acc + e_ref[...])
        elif epilogue == "residual":
            acc = acc + e_ref[...]
        o_ref[...] = acc.astype(o_ref.dtype)


def matmul(a, b, out_dtype, epilogue="none", extra=None, tm=1024, tn=1024, tk=1024):
    m, kdim = a.shape
    _, n = b.shape
    tm, tn, tk = _tile(m, tm), _tile(n, tn), _tile(kdim, tk)
    nk = kdim // tk
    in_specs = [pl.BlockSpec((tm, tk), lambda i, j, k: (i, k)),
                pl.BlockSpec((tk, tn), lambda i, j, k: (k, j))]
    args = [a, b]
    if epilogue == "sigmoid_bias":
        in_specs.append(pl.BlockSpec((1, tn), lambda i, j, k: (0, j)))
        args.append(extra.reshape(1, n).astype(F32))
    elif epilogue == "residual":
        in_specs.append(pl.BlockSpec((tm, tn), lambda i, j, k: (i, j)))
        args.append(extra)
    return pl.pallas_call(
        functools.partial(_mm_kernel, epilogue=epilogue, nk=nk),
        out_shape=jax.ShapeDtypeStruct((m, n), out_dtype),
        grid=(m // tm, n // tn, nk),
        in_specs=in_specs,
        out_specs=pl.BlockSpec((tm, tn), lambda i, j, k: (i, j)),
        scratch_shapes=[pltpu.VMEM((tm, tn), F32)],
        compiler_params=_cparams(("parallel", "parallel", "arbitrary")),
        name="matmul_" + epilogue,
    )(*args)


def _mix_kernel(osb_ref, omb_ref, wsb_ref, wmb_ref, g0_ref, g1_ref, o_ref):
    u_sb = jnp.dot(osb_ref[...], wsb_ref[...], preferred_element_type=F32)
    u_mb = jnp.dot(omb_ref[...], wmb_ref[...], preferred_element_type=F32)
    mixed = g0_ref[...].astype(F32) * u_sb + g1_ref[...].astype(F32) * u_mb
    o_ref[...] = mixed.astype(o_ref.dtype)


def branch_mix(o_sb, o_mb, w_sb, w_mb, gates, tm=512, tn=1024):
    m, kdim = o_sb.shape
    d = w_sb.shape[1]
    tm, tn = _tile(m, tm), _tile(d, tn)
    nj = d // tn
    return pl.pallas_call(
        _mix_kernel,
        out_shape=jax.ShapeDtypeStruct((m, d), BF16),
        grid=(m // tm, nj),
        in_specs=[pl.BlockSpec((tm, kdim), lambda i, j: (i, 0)),
                  pl.BlockSpec((tm, kdim), lambda i, j: (i, 0)),
                  pl.BlockSpec((kdim, tn), lambda i, j: (0, j)),
                  pl.BlockSpec((kdim, tn), lambda i, j: (0, j)),
                  pl.BlockSpec((tm, tn), lambda i, j: (i, j)),
                  pl.BlockSpec((tm, tn), lambda i, j: (i, nj + j))],
        out_specs=pl.BlockSpec((tm, tn), lambda i, j: (i, j)),
        compiler_params=_cparams(("parallel", "parallel")),
        name="branch_mix",
    )(o_sb, o_mb, w_sb, w_mb, gates, gates)


def _softplus(z):
    return jnp.maximum(z, 0.0) + jnp.log(1.0 + jnp.exp(-jnp.abs(z)))


def _sb_kernel(q_ref, k_ref, v_ref, u_ref, o_ref, acc_ref, run_ref, *, tq, scale):
    i = pl.program_id(2)
    q = (q_ref[...].astype(F32) * scale).astype(BF16)
    row = lax.broadcasted_iota(jnp.int32, (tq, tq), 0)
    col = lax.broadcasted_iota(jnp.int32, (tq, tq), 1)
    past = col < row

    def block(j, diagonal):
        start = pl.multiple_of(j * tq, tq)
        kb = k_ref[pl.ds(start, tq), :]
        vb = v_ref[pl.ds(start, tq), :]
        z = lax.dot_general(q, kb, (((1,), (1,)), ((), ())), preferred_element_type=F32)
        sp = _softplus(z)
        if diagonal:
            sp = jnp.where(past, sp, 0.0)
        c = jnp.dot(sp.astype(BF16), u_ref[...], preferred_element_type=F32)
        run = run_ref[...]
        later = jnp.concatenate([run] * (tq // LANES), axis=1)
        w = jnp.exp(z - c[:, :tq] - later)
        if diagonal:
            w = jnp.where(past, w, 0.0)
        pv = jnp.dot(w.astype(BF16), vb, preferred_element_type=F32)
        if diagonal:
            acc_ref[...] = pv
            run_ref[...] = c[:, tq:]
        else:
            acc_ref[...] += pv
            run_ref[...] = run + c[:, tq:]

    run_ref[...] = jnp.zeros_like(run_ref)
    block(i, True)

    def body(jj, carry):
        block(i - 1 - jj, False)
        return carry

    lax.fori_loop(0, i, body, 0)
    o_ref[...] = acc_ref[...].astype(o_ref.dtype)


def sb_attention(proj, batch, seq, n_heads, q_col, k_col, v_col, tq=256):
    d = HEAD_DIM
    tq = _tile(seq, tq)
    nq = seq // tq
    scale = 1.0 / math.sqrt(d)
    r = jnp.arange(tq)
    tri = (r[:, None] >= r[None, :]).astype(BF16)
    u = jnp.concatenate([tri, jnp.ones((tq, LANES), BF16)], axis=1)
    return pl.pallas_call(
        functools.partial(_sb_kernel, tq=tq, scale=scale),
        out_shape=jax.ShapeDtypeStruct((batch * seq, n_heads * d), BF16),
        grid=(batch, n_heads, nq),
        in_specs=[pl.BlockSpec((tq, d), lambda b, h, i: (b * nq + i, q_col + h)),
                  pl.BlockSpec((seq, d), lambda b, h, i: (b, k_col + h)),
                  pl.BlockSpec((seq, d), lambda b, h, i: (b, v_col + h)),
                  pl.BlockSpec((tq, tq + LANES), lambda b, h, i: (0, 0))],
        out_specs=pl.BlockSpec((tq, d), lambda b, h, i: (b * nq + i, h)),
        scratch_shapes=[pltpu.VMEM((tq, d), F32), pltpu.VMEM((tq, LANES), F32)],
        compiler_params=_cparams(("parallel", "parallel", "arbitrary")),
        name="sb_attention",
    )(proj, proj, proj, u)


def _moba_kernel(slope_ref, q_ref, k_ref, v_ref, pool_ref, o_ref,
                 kmean_ref, pen_ref, m_ref, l_ref, acc_ref, *, blk, scale, n_blocks):
    i = pl.program_id(2)

    @pl.when(i == 0)
    def _():
        kmean_ref[...] = jnp.dot(pool_ref[...], k_ref[...], preferred_element_type=F32)

    q_raw = q_ref[...]
    q = (q_raw.astype(F32) * scale).astype(BF16)

    km = kmean_ref[...]
    km_hi = km.astype(BF16)
    km_lo = (km - km_hi.astype(F32)).astype(BF16)
    dn = (((1,), (1,)), ((), ()))
    gate = (lax.dot_general(q_raw, km_hi, dn, preferred_element_type=F32)
            + lax.dot_general(q_raw, km_lo, dn, preferred_element_type=F32))
    lane = lax.broadcasted_iota(jnp.int32, (blk, LANES), 1)
    gate = jnp.where(lane < i, gate, -jnp.inf)
    pen = jnp.full((blk, LANES), NEG_BIG, F32)
    for r in range(MOBA_TOPK):
        best = jnp.max(gate, axis=-1, keepdims=True)
        first = jnp.min(jnp.where(gate == best, lane, LANES), axis=-1, keepdims=True)
        hit = (lane == first) & (lane < i)
        pen = jnp.where(hit, 0.0, pen)
        gate = jnp.where(lane == first, -jnp.inf, gate)
    pen_ref[...] = pen

    slope = slope_ref[0]
    col_f = lax.broadcasted_iota(jnp.int32, (1, blk), 1).astype(F32)
    row = lax.broadcasted_iota(jnp.int32, (blk, blk), 0)
    col = lax.broadcasted_iota(jnp.int32, (blk, blk), 1)

    def block(j, own):
        start = pl.multiple_of(j * blk, blk)
        kb = k_ref[pl.ds(start, blk), :]
        vb = v_ref[pl.ds(start, blk), :]
        s = lax.dot_general(q, kb, dn, preferred_element_type=F32)
        key_off = ((j - i) * blk).astype(F32)
        s = s + slope * (col_f + key_off)
        if own:
            s = jnp.where(col <= row, s, NEG_BIG)
            m_new = jnp.max(s, axis=-1, keepdims=True)
            p = jnp.exp(s - m_new)
            l_ref[...] = jnp.sum(p, axis=-1, keepdims=True)
            acc_ref[...] = jnp.dot(p.astype(BF16), vb, preferred_element_type=F32)
            m_ref[...] = m_new
        else:
            lane_j = lax.broadcasted_iota(jnp.int32, (blk, LANES), 1)
            row_pen = jnp.sum(jnp.where(lane_j == j, pen_ref[...], 0.0), axis=-1, keepdims=True)
            s = s + row_pen
            m_old = m_ref[...]
            m_new = jnp.maximum(m_old, jnp.max(s, axis=-1, keepdims=True))
            alpha = jnp.exp(m_old - m_new)
            p = jnp.exp(s - m_new)
            l_ref[...] = alpha * l_ref[...] + jnp.sum(p, axis=-1, keepdims=True)
            acc_ref[...] = alpha * acc_ref[...] + jnp.dot(p.astype(BF16), vb, preferred_element_type=F32)
            m_ref[...] = m_new

    block(i, True)

    def body(jj, carry):
        block(i - 1 - jj, False)
        return carry

    lax.fori_loop(0, i, body, 0)
    o_ref[...] = (acc_ref[...] / l_ref[...]).astype(o_ref.dtype)


def moba_attention(proj, batch, seq, n_heads, q_col, k_col, v_col):
    d = HEAD_DIM
    blk = MOBA_BLOCK
    assert seq % blk == 0 and seq // blk <= LANES
    nb = seq // blk
    scale = 1.0 / math.sqrt(d)
    slopes = jnp.exp2(-8.0 * jnp.arange(1, n_heads + 1, dtype=F32) / n_heads)
    slope_arr = jnp.broadcast_to(slopes[:, None, None], (n_heads, 1, blk))
    pool = ((jnp.arange(seq)[None, :] // blk) == jnp.arange(LANES)[:, None]).astype(BF16) * (1.0 / blk)
    pool = pool.astype(BF16)
    return pl.pallas_call(
        functools.partial(_moba_kernel, blk=blk, scale=scale, n_blocks=nb),
        out_shape=jax.ShapeDtypeStruct((batch * seq, n_heads * d), BF16),
        grid=(batch, n_heads, nb),
        in_specs=[pl.BlockSpec((1, 1, blk), lambda b, h, i: (h, 0, 0)),
                  pl.BlockSpec((blk, d), lambda b, h, i: (b * nb + i, q_col + h)),
                  pl.BlockSpec((seq, d), lambda b, h, i: (b, k_col + h)),
                  pl.BlockSpec((seq, d), lambda b, h, i: (b, v_col + h)),
                  pl.BlockSpec((LANES, seq), lambda b, h, i: (0, 0))],
        out_specs=pl.BlockSpec((blk, d), lambda b, h, i: (b * nb + i, h)),
        scratch_shapes=[pltpu.VMEM((LANES, d), F32), pltpu.VMEM((blk, LANES), F32),
                        pltpu.VMEM((blk, 1), F32), pltpu.VMEM((blk, 1), F32),
                        pltpu.VMEM((blk, d), F32)],
        compiler_params=_cparams(("parallel", "parallel", "arbitrary")),
        name="moba_attention",
    )(slope_arr, proj, proj, proj, pool)


def _split(x):
    hi = x.astype(BF16)
    lo = (x - hi.astype(F32)).astype(BF16)
    return hi, lo


def _router_kernel(x_ref, g_ref, w_ref, b_ref, idx_ref, wgt_ref, *, n_experts):
    h = _rms(x_ref[...], g_ref[...])
    h_hi, h_lo = _split(h)
    w_hi, w_lo = _split(w_ref[...])
    logits = (jnp.dot(h_hi, w_hi, preferred_element_type=F32)
              + jnp.dot(h_lo, w_hi, preferred_element_type=F32)
              + jnp.dot(h_hi, w_lo, preferred_element_type=F32)) + b_ref[...]
    tm = logits.shape[0]
    lane = lax.broadcasted_iota(jnp.int32, (tm, LANES), 1)
    logits = jnp.where(lane < n_experts, logits, -jnp.inf)
    idx_out = jnp.zeros((tm, LANES), jnp.int32)
    val_out = jnp.full((tm, LANES), -jnp.inf, F32)
    top = None
    for r in range(TOP_K):
        best = jnp.max(logits, axis=-1, keepdims=True)
        first = jnp.min(jnp.where(logits == best, lane, LANES), axis=-1, keepdims=True)
        if r == 0:
            top = best
        idx_out = jnp.where(lane == r, first, idx_out)
        val_out = jnp.where(lane == r, best, val_out)
        logits = jnp.where(lane == first, -jnp.inf, logits)
    e = jnp.exp(val_out - top)
    idx_ref[...] = idx_out
    wgt_ref[...] = e / jnp.sum(e, axis=-1, keepdims=True)


def router(x, g, w_router, b_router, tm=256):
    n, d = x.shape
    n_experts = w_router.shape[1]
    tm = _tile(n, tm, 8)
    w_pad = jnp.zeros((d, LANES), F32).at[:, :n_experts].set(w_router)
    b_pad = jnp.zeros((1, LANES), F32).at[0, :n_experts].set(b_router)
    idx, wgt = pl.pallas_call(
        functools.partial(_router_kernel, n_experts=n_experts),
        out_shape=(jax.ShapeDtypeStruct((n, LANES), jnp.int32),
                   jax.ShapeDtypeStruct((n, LANES), F32)),
        grid=(n // tm,),
        in_specs=[pl.BlockSpec((tm, d), lambda i: (i, 0)),
                  pl.BlockSpec((1, d), lambda i: (0, 0)),
                  pl.BlockSpec((d, LANES), lambda i: (0, 0)),
                  pl.BlockSpec((1, LANES), lambda i: (0, 0))],
        out_specs=(pl.BlockSpec((tm, LANES), lambda i: (i, 0)),
                   pl.BlockSpec((tm, LANES), lambda i: (i, 0))),
        compiler_params=_cparams(("parallel",)),
        name="router",
    )(x, g.reshape(1, d).astype(F32), w_pad, b_pad)
    return idx[:, :TOP_K], wgt[:, :TOP_K]


def _row_copy(src_hbm, dst_vmem, sem, src_row, dst_row):
    return pltpu.make_async_copy(src_hbm.at[pl.ds(src_row, 1), :],
                                 dst_vmem.at[pl.ds(dst_row, 1), :], sem)


def _gather_kernel(src_ref, x_hbm, g_ref, o_ref, buf_ref, sem, *, rows):
    def start(r, c):
        _row_copy(x_hbm, buf_ref, sem, src_ref[0, 0, r], r).start()
        return c

    lax.fori_loop(0, rows, start, 0)

    def wait(r, c):
        _row_copy(x_hbm, buf_ref, sem, 0, r).wait()
        return c

    lax.fori_loop(0, rows, wait, 0)
    o_ref[...] = _rms(buf_ref[...], g_ref[...]).astype(o_ref.dtype)


def gather_rows_rmsnorm(x, g, src, rows=256):
    n, d = x.shape
    p = src.shape[0]
    rows = min(rows, p)
    nt = p // rows
    return pl.pallas_call(
        functools.partial(_gather_kernel, rows=rows),
        out_shape=jax.ShapeDtypeStruct((p, d), BF16),
        grid=(nt,),
        in_specs=[pl.BlockSpec((1, 1, rows), lambda i: (i, 0, 0), memory_space=pltpu.SMEM),
                  pl.BlockSpec(memory_space=pl.ANY),
                  pl.BlockSpec((1, d), lambda i: (0, 0))],
        out_specs=pl.BlockSpec((rows, d), lambda i: (i, 0)),
        scratch_shapes=[pltpu.VMEM((rows, d), F32), pltpu.SemaphoreType.DMA(())],
        compiler_params=_cparams(("arbitrary",)),
        name="moe_gather",
    )(src.reshape(nt, 1, rows), x, g.reshape(1, d).astype(F32))


def _new_expert(te_ref, t):
    return jnp.logical_or(t == 0, te_ref[t] != te_ref[jnp.maximum(t - 1, 0)])


def _gate_up_kernel(te_ref, nv_ref, x_ref, wg_ref, wu_ref, bg_ref, bu_ref, o_ref, wgb_ref, wub_ref):
    t = pl.program_id(1)

    @pl.when(t < nv_ref[0])
    def _():
        @pl.when(_new_expert(te_ref, t))
        def _():
            wgb_ref[...] = wg_ref[0].astype(BF16)
            wub_ref[...] = wu_ref[0].astype(BF16)

        x = x_ref[...]
        g = jnp.dot(x, wgb_ref[...], preferred_element_type=F32) + bg_ref[0]
        u = jnp.dot(x, wub_ref[...], preferred_element_type=F32) + bu_ref[0]
        g = jnp.minimum(g, SWIGLU_LIMIT)
        u = jnp.clip(u, -SWIGLU_LIMIT, SWIGLU_LIMIT)
        o_ref[...] = (g * jax.nn.sigmoid(SWIGLU_ALPHA * g) * (u + 1.0)).astype(o_ref.dtype)

    @pl.when(t >= nv_ref[0])
    def _():
        o_ref[...] = jnp.zeros_like(o_ref)


def expert_gate_up(xs, tile_expert, n_valid, w_gate_up, b_gate_up, tm, tf=256):
    p, d = xs.shape
    n_e, _, f2 = w_gate_up.shape
    f = f2 // 2
    tf = _tile(f, tf)
    nf = f // tf
    nt = p // tm
    b3 = b_gate_up.reshape(n_e, 1, f2)

    def row_tile(c, t, te, nv):
        return jnp.minimum(t, nv[0] - 1)

    grid_spec = pltpu.PrefetchScalarGridSpec(
        num_scalar_prefetch=2,
        grid=(nf, nt),
        in_specs=[pl.BlockSpec((tm, d), lambda c, t, te, nv: (row_tile(c, t, te, nv), 0)),
                  pl.BlockSpec((1, d, tf), lambda c, t, te, nv: (te[t], 0, c)),
                  pl.BlockSpec((1, d, tf), lambda c, t, te, nv: (te[t], 0, nf + c)),
                  pl.BlockSpec((1, 1, tf), lambda c, t, te, nv: (te[t], 0, c)),
                  pl.BlockSpec((1, 1, tf), lambda c, t, te, nv: (te[t], 0, nf + c))],
        out_specs=pl.BlockSpec((tm, tf), lambda c, t, te, nv: (t, c)),
        scratch_shapes=[pltpu.VMEM((d, tf), BF16), pltpu.VMEM((d, tf), BF16)],
    )
    return pl.pallas_call(
        _gate_up_kernel,
        out_shape=jax.ShapeDtypeStruct((p, f), BF16),
        grid_spec=grid_spec,
        compiler_params=_cparams(("arbitrary", "arbitrary")),
        name="moe_gate_up",
    )(tile_expert, n_valid, xs, w_gate_up, w_gate_up, b3, b3)


def _down_kernel(te_ref, nv_ref, h_ref, w_ref, b_ref, rw_ref, o_ref, wb_ref):
    t = pl.program_id(1)

    @pl.when(t < nv_ref[0])
    def _():
        @pl.when(_new_expert(te_ref, t))
        def _():
            wb_ref[...] = w_ref[0].astype(BF16)

        y = jnp.dot(h_ref[...], wb_ref[...], preferred_element_type=F32) + b_ref[0]
        o_ref[...] = (y * rw_ref[...]).astype(o_ref.dtype)

    @pl.when(t >= nv_ref[0])
    def _():
        o_ref[...] = jnp.zeros_like(o_ref)


def expert_down(hid, tile_expert, n_valid, w_down, b_down, row_w, tm, tn=1024):
    p, f = hid.shape
    n_e, _, d = w_down.shape
    tn = _tile(d, tn)
    nd = d // tn
    nt = p // tm
    b3 = b_down.reshape(n_e, 1, d)

    def row_tile(c, t, te, nv):
        return jnp.minimum(t, nv[0] - 1)

    grid_spec = pltpu.PrefetchScalarGridSpec(
        num_scalar_prefetch=2,
        grid=(nd, nt),
        in_specs=[pl.BlockSpec((tm, f), lambda c, t, te, nv: (row_tile(c, t, te, nv), 0)),
                  pl.BlockSpec((1, f, tn), lambda c, t, te, nv: (te[t], 0, c)),
                  pl.BlockSpec((1, 1, tn), lambda c, t, te, nv: (te[t], 0, c)),
                  pl.BlockSpec((tm, 1), lambda c, t, te, nv: (row_tile(c, t, te, nv), 0))],
        out_specs=pl.BlockSpec((tm, tn), lambda c, t, te, nv: (t, c)),
        scratch_shapes=[pltpu.VMEM((f, tn), BF16)],
    )
    return pl.pallas_call(
        _down_kernel,
        out_shape=jax.ShapeDtypeStruct((p, d), F32),
        grid_spec=grid_spec,
        compiler_params=_cparams(("arbitrary", "arbitrary")),
        name="moe_down",
    )(tile_expert, n_valid, hid, w_down, b3, row_w.reshape(p, 1))


def _combine_kernel(pos_ref, x_ref, y_hbm, g_ref, o_ref, buf_ref, sem, *, tt):
    def start(r, c):
        for k in range(TOP_K):
            _row_copy(y_hbm, buf_ref.at[k], sem, pos_ref[0, 0, r * TOP_K + k], r).start()
        return c

    lax.fori_loop(0, tt, start, 0)

    def wait(r, c):
        for k in range(TOP_K):
            _row_copy(y_hbm, buf_ref.at[k], sem, 0, r).wait()
        return c

    lax.fori_loop(0, tt, wait, 0)
    acc = x_ref[...]
    for k in range(TOP_K):
        acc = acc + buf_ref[k]
    o_ref[...] = _rms(acc, g_ref[...]).astype(o_ref.dtype)


def combine(x, y, pos, g, tt=128):
    n, d = x.shape
    tt = _tile(n, tt, 8)
    nt = n // tt
    return pl.pallas_call(
        functools.partial(_combine_kernel, tt=tt),
        out_shape=jax.ShapeDtypeStruct((n, d), F32),
        grid=(nt,),
        in_specs=[pl.BlockSpec((1, 1, tt * TOP_K), lambda i: (i, 0, 0), memory_space=pltpu.SMEM),
                  pl.BlockSpec((tt, d), lambda i: (i, 0)),
                  pl.BlockSpec(memory_space=pl.ANY),
                  pl.BlockSpec((1, d), lambda i: (0, 0))],
        out_specs=pl.BlockSpec((tt, d), lambda i: (i, 0)),
        scratch_shapes=[pltpu.VMEM((TOP_K, tt, d), F32), pltpu.SemaphoreType.DMA(())],
        compiler_params=_cparams(("arbitrary",)),
        name="moe_combine",
    )(pos.reshape(nt, 1, tt * TOP_K), x, y, g.reshape(1, d).astype(F32))


def route_plan(top_idx, top_w, n_experts, tm):
    n = top_idx.shape[0]
    pairs = n * TOP_K
    p_rows = pairs + n_experts * tm
    e_flat = top_idx.reshape(pairs)
    onehot = (e_flat[:, None] == jnp.arange(n_experts)[None, :]).astype(jnp.int32)
    csum = jnp.cumsum(onehot, axis=0)
    rank = jnp.take_along_axis(csum, e_flat[:, None], axis=1)[:, 0] - 1
    counts = csum[-1]
    padded = ((counts + tm - 1) // tm) * tm
    ends = jnp.cumsum(padded)
    starts = ends - padded
    pos = starts[e_flat] + rank
    token = jnp.arange(pairs, dtype=jnp.int32) // TOP_K
    src = jnp.zeros((p_rows,), jnp.int32).at[pos].set(token)
    row_w = jnp.zeros((p_rows,), F32).at[pos].set(top_w.reshape(pairs))
    n_tiles = p_rows // tm
    tile_start = jnp.arange(n_tiles, dtype=jnp.int32) * tm
    tile_expert = jnp.minimum(jnp.searchsorted(ends, tile_start, side="right"), n_experts - 1)
    n_valid = (ends[-1] // tm).astype(jnp.int32).reshape(1)
    tile_expert = jnp.where(tile_start < ends[-1], tile_expert, tile_expert[jnp.maximum(n_valid[0] - 1, 0)])
    return pos.astype(jnp.int32), src, row_w, tile_expert.astype(jnp.int32), n_valid


def moe_block(x1, norm_ffn, w_router, b_router, w_gate_up, b_gate_up, w_down, b_down, norm_final, tm=512):
    n_experts = w_router.shape[1]
    top_idx, top_w = router(x1, norm_ffn, w_router, b_router)
    pos, src, row_w, tile_expert, n_valid = route_plan(top_idx, top_w, n_experts, tm)
    xs = gather_rows_rmsnorm(x1, norm_ffn, src)
    hid = expert_gate_up(xs, tile_expert, n_valid, w_gate_up, b_gate_up, tm)
    y = expert_down(hid, tile_expert, n_valid, w_down, b_down, row_w, tm)
    return combine(x1, y, pos, norm_final)


def kernel(x, norm_mix, w_in, w_up_sb, w_up_moba, w_branch_gate, b_branch_gate, w_out, norm_ffn,
           w_router, b_router, w_gate_up, b_gate_up, w_down, b_down, norm_final):
    b, t, d = x.shape
    n = b * t
    assert norm_mix.shape[0] == 1, "single-layer block"
    x2 = x.reshape(n, d)
    c_sb = w_up_sb.shape[1] // HEAD_DIM
    c_mb = w_up_moba.shape[1] // HEAD_DIM
    h = rmsnorm(x2, norm_mix[0], BF16)
    proj = matmul(h, w_in[0].astype(BF16), BF16)
    gates = matmul(h, w_branch_gate[0].astype(BF16), BF16, "sigmoid_bias", b_branch_gate[0])
    o_sb = sb_attention(proj, b, t, c_sb, 0, c_sb, 2 * c_sb)
    o_mb = moba_attention(proj, b, t, c_mb, 3 * c_sb, 3 * c_sb + c_mb, 3 * c_sb + 2 * c_mb)
    mixed = branch_mix(o_sb, o_mb, w_up_sb[0].astype(BF16), w_up_moba[0].astype(BF16), gates)
    x1 = matmul(mixed, w_out[0].astype(BF16), F32, "residual", x2)
    out = moe_block(x1, norm_ffn[0], w_router[0], b_router[0], w_gate_up[0], b_gate_up[0],
                    w_down[0], b_down[0], norm_final)
    return out.reshape(b, t, d)
```

```python
import functools
import math

import jax
import jax.numpy as jnp
import numpy as np
from jax import lax
from jax.experimental import pallas as pl
from jax.experimental.pallas import tpu as pltpu

F32 = jnp.float32
BF16 = jnp.bfloat16

HEAD_DIM = 128
MOBA_BLOCK = 256
MOBA_TOPK = 3
TOP_K = 4
SWIGLU_ALPHA = 1.702
SWIGLU_LIMIT = 7.0
RMS_EPS = 1e-5
LOG2E = 1.4426950408889634
LANES = 128
NEG_BIG = -1e30
VMEM_LIMIT = 56 * 1024 * 1024
N_PEN_COLS = 32
AUG_BIAS_COLS = 6
POS_SPLIT = 64


def _bf16_pieces(x, n):
    out, r = [], np.asarray(x, np.float32)
    for _ in range(n):
        piece = (r.view(np.uint32) & np.uint32(0xFFFF0000)).view(np.float32)
        out.append(piece)
        r = (r - piece).astype(np.float32)
    return out


def _cparams(sem):
    return pltpu.CompilerParams(dimension_semantics=sem, vmem_limit_bytes=VMEM_LIMIT)


def _tile(dim, target, align=LANES):
    if dim <= target:
        return dim
    t = (target // align) * align
    while t > align and dim % t:
        t -= align
    assert dim % t == 0, (dim, target)
    return t


def _rms(x, g):
    ms = jnp.mean(x * x, axis=-1, keepdims=True)
    return x * lax.rsqrt(ms + RMS_EPS) * g


def _rmsnorm_kernel(x_ref, g_ref, o_ref):
    o_ref[...] = _rms(x_ref[...].astype(F32), g_ref[...]).astype(o_ref.dtype)


def rmsnorm(x, g, out_dtype, tm=256):
    n, d = x.shape
    tm = _tile(n, tm, 8)
    return pl.pallas_call(
        _rmsnorm_kernel,
        out_shape=jax.ShapeDtypeStruct((n, d), out_dtype),
        grid=(n // tm,),
        in_specs=[pl.BlockSpec((tm, d), lambda i: (i, 0)),
                  pl.BlockSpec((1, d), lambda i: (0, 0))],
        out_specs=pl.BlockSpec((tm, d), lambda i: (i, 0)),
        compiler_params=_cparams(("parallel",)),
        name="rmsnorm",
    )(x, g.reshape(1, d).astype(F32))


def _mm_kernel(*refs, epilogue, nk):
    if epilogue == "none":
        a_ref, b_ref, o_ref, acc_ref = refs
        e_ref = None
    else:
        a_ref, b_ref, e_ref, o_ref, acc_ref = refs
    k = pl.program_id(2)

    @pl.when(k == 0)
    def _():
        acc_ref[...] = jnp.zeros_like(acc_ref)

    acc_ref[...] += jnp.dot(a_ref[...], b_ref[...], preferred_element_type=F32)

    @pl.when(k == nk - 1)
    def _():
        acc = acc_ref[...]
        if epilogue == "sigmoid_bias":
            acc = jax.nn.sigmoid(acc + e_ref[...])
        elif epilogue == "residual":
            acc = acc + e_ref[...]
        o_ref[...] = acc.astype(o_ref.dtype)


def matmul(a, b, out_dtype, epilogue="none", extra=None, tm=1024, tn=1024, tk=1024):
    m, kdim = a.shape
    _, n = b.shape
    tm, tn, tk = _tile(m, tm), _tile(n, tn), _tile(kdim, tk)
    nk = kdim // tk
    in_specs = [pl.BlockSpec((tm, tk), lambda i, j, k: (i, k)),
                pl.BlockSpec((tk, tn), lambda i, j, k: (k, j))]
    args = [a, b]
    if epilogue == "sigmoid_bias":
        in_specs.append(pl.BlockSpec((1, tn), lambda i, j, k: (0, j)))
        args.append(extra.reshape(1, n).astype(F32))
    elif epilogue == "residual":
        in_specs.append(pl.BlockSpec((tm, tn), lambda i, j, k: (i, j)))
        args.append(extra)
    return pl.pallas_call(
        functools.partial(_mm_kernel, epilogue=epilogue, nk=nk),
        out_shape=jax.ShapeDtypeStruct((m, n), out_dtype),
        grid=(m // tm, n // tn, nk),
        in_specs=in_specs,
        out_specs=pl.BlockSpec((tm, tn), lambda i, j, k: (i, j)),
        scratch_shapes=[pltpu.VMEM((tm, tn), F32)],
        compiler_params=_cparams(("parallel", "parallel", "arbitrary")),
        name="matmul_" + epilogue,
    )(*args)


def _mix_kernel(osb_ref, omb_ref, wsb_ref, wmb_ref, g0_ref, g1_ref, o_ref):
    u_sb = jnp.dot(osb_ref[...], wsb_ref[...], preferred_element_type=F32)
    u_mb = jnp.dot(omb_ref[...], wmb_ref[...], preferred_element_type=F32)
    mixed = g0_ref[...].astype(F32) * u_sb + g1_ref[...].astype(F32) * u_mb
    o_ref[...] = mixed.astype(o_ref.dtype)


def branch_mix(o_sb, o_mb, w_sb, w_mb, gates, tm=512, tn=1024):
    m, kdim = o_sb.shape
    d = w_sb.shape[1]
    tm, tn = _tile(m, tm), _tile(d, tn)
    nj = d // tn
    return pl.pallas_call(
        _mix_kernel,
        out_shape=jax.ShapeDtypeStruct((m, d), BF16),
        grid=(m // tm, nj),
        in_specs=[pl.BlockSpec((tm, kdim), lambda i, j: (i, 0)),
                  pl.BlockSpec((tm, kdim), lambda i, j: (i, 0)),
                  pl.BlockSpec((kdim, tn), lambda i, j: (0, j)),
                  pl.BlockSpec((kdim, tn), lambda i, j: (0, j)),
                  pl.BlockSpec((tm, tn), lambda i, j: (i, j)),
                  pl.BlockSpec((tm, tn), lambda i, j: (i, nj + j))],
        out_specs=pl.BlockSpec((tm, tn), lambda i, j: (i, j)),
        compiler_params=_cparams(("parallel", "parallel")),
        name="branch_mix",
    )(o_sb, o_mb, w_sb, w_mb, gates, gates)


def _softplus2(z):
    bits = lax.bitcast_convert_type(z, jnp.uint32) | jnp.uint32(0x80000000)
    neg_abs = lax.bitcast_convert_type(bits, F32)
    return jnp.maximum(z, 0.0) + jnp.log2(1.0 + jnp.exp2(neg_abs))


def _sb_kernel(q_ref, k_ref, v_ref, u_ref, o_ref, acc_ref, run_ref, *, tq, tk, nblk, scale):
    i = pl.program_id(2)
    n_sub = tq // tk
    q = (q_ref[...].astype(F32) * (scale * LOG2E)).astype(BF16)
    dn = (((1,), (1,)), ((), ()))

    def group(j_hi, n, masked):
        lo = j_hi - (n - 1)
        start = pl.multiple_of(lo * tk, tk)
        kb = k_ref[pl.ds(start, n * tk), :]
        vb = v_ref[pl.ds(start, n * tk), :]
        z = lax.dot_general(q, kb, dn, preferred_element_type=F32)
        sp = _softplus2(z)
        if masked:
            row = lax.broadcasted_iota(jnp.int32, (tq, n * tk), 0) + i * tq
            col = lax.broadcasted_iota(jnp.int32, (tq, n * tk), 1) + lo * tk
            past = col < row
            sp = jnp.where(past, sp, 0.0)
        spb = sp.astype(BF16)
        run = run_ref[...]
        parts = [None] * n
        for b in range(n - 1, -1, -1):
            c = jnp.dot(spb[:, b * tk:(b + 1) * tk], u_ref[...], preferred_element_type=F32)
            later = jnp.concatenate([run] * (tk // LANES), axis=1)
            parts[b] = c + later
            run = run + jnp.broadcast_to(c[:, 0:1], run.shape)
        w = jnp.exp2(z - jnp.concatenate(parts, axis=1))
        if masked:
            w = jnp.where(past, w, 0.0)
        acc_ref[...] += jnp.dot(w.astype(BF16), vb, preferred_element_type=F32)
        run_ref[...] = run

    run_ref[...] = jnp.zeros_like(run_ref)
    acc_ref[...] = jnp.zeros_like(acc_ref)

    def diag_body(g, carry):
        group((i + 1) * n_sub - 1 - g * nblk, nblk, True)
        return carry

    lax.fori_loop(0, n_sub // nblk, diag_body, 0)

    def body(g, carry):
        group(i * n_sub - 1 - g * nblk, nblk, False)
        return carry

    lax.fori_loop(0, (i * n_sub) // nblk, body, 0)
    o_ref[...] = acc_ref[...].astype(o_ref.dtype)


def sb_attention(proj, batch, seq, n_heads, q_col, k_col, v_col, tq=1024, tk=256, nblk=2):
    d = HEAD_DIM
    tq = _tile(seq, tq)
    tk = _tile(tq, tk)
    nq = seq // tq
    assert (tq // tk) % nblk == 0
    scale = 1.0 / math.sqrt(d)
    r = jnp.arange(tk)
    u = (r[:, None] >= r[None, :]).astype(BF16)
    return pl.pallas_call(
        functools.partial(_sb_kernel, tq=tq, tk=tk, nblk=nblk, scale=scale),
        out_shape=jax.ShapeDtypeStruct((batch * seq, n_heads * d), BF16),
        grid=(batch, n_heads, nq),
        in_specs=[pl.BlockSpec((tq, d), lambda b, h, i: (b * nq + i, q_col + h)),
                  pl.BlockSpec((seq, d), lambda b, h, i: (b, k_col + h)),
                  pl.BlockSpec((seq, d), lambda b, h, i: (b, v_col + h)),
                  pl.BlockSpec((tk, tk), lambda b, h, i: (0, 0))],
        out_specs=pl.BlockSpec((tq, d), lambda b, h, i: (b * nq + i, h)),
        scratch_shapes=[pltpu.VMEM((tq, d), F32), pltpu.VMEM((tq, LANES), F32)],
        compiler_params=_cparams(("parallel", "parallel", "arbitrary")),
        name="sb_attention",
    )(proj, proj, proj, u)


def _moba_kernel(q_ref, k_ref, v_ref, aug_ref, cst_ref, pool_ref, o_ref,
                 kaug_ref, vaug_ref, kmean_ref, acc_ref, m_ref, s0_ref, s1_ref, *, blk, tq, tkc, scale):
    i = pl.program_id(2)
    d = HEAD_DIM

    @pl.when(i == 0)
    def _():
        kaug_ref[:, :d] = k_ref[...]
        kaug_ref[:, d:] = aug_ref[...]
        vaug_ref[:, :d] = v_ref[...]
        vaug_ref[:, d:] = jnp.ones((vaug_ref.shape[0], d), BF16)
        kmean_ref[...] = jnp.dot(pool_ref[...], k_ref[...], preferred_element_type=F32)

    q_raw = q_ref[...]
    dn = (((1,), (1,)), ((), ()))

    km = kmean_ref[...]
    km_hi = km.astype(BF16)
    km_lo = (km - km_hi.astype(F32)).astype(BF16)
    gate = (lax.dot_general(q_raw, km_hi, dn, preferred_element_type=F32)
            + lax.dot_general(q_raw, km_lo, dn, preferred_element_type=F32))
    lane = lax.broadcasted_iota(jnp.int32, (tq, LANES), 1)
    row_blk = lax.shift_right_logical(lax.broadcasted_iota(jnp.int32, (tq, LANES), 0), int(math.log2(blk)))
    own = i * (tq // blk) + row_blk
    is_past = lane < own
    gate = jnp.where(is_past, gate, -jnp.inf)
    pen = jnp.where(is_past, NEG_BIG, 0.0)
    for r in range(MOBA_TOPK):
        best = jnp.max(gate, axis=-1, keepdims=True)
        first = jnp.min(jnp.where(gate == best, lane, LANES), axis=-1, keepdims=True)
        hit = (lane == first) & is_past
        pen = jnp.where(hit, 0.0, pen)
        gate = jnp.where(lane == first, -jnp.inf, gate)

    qs = (q_raw.astype(F32) * (scale * LOG2E)).astype(BF16)
    aux = jnp.where(lane < N_PEN_COLS, pen, cst_ref[0])
    q_aug = jnp.concatenate([qs, aux.astype(BF16)], axis=1)

    def logits(c, s_ref, diagonal):
        start = pl.multiple_of(c * tkc, tkc)
        kb = kaug_ref[pl.ds(start, tkc), :]
        s = lax.dot_general(q_aug, kb, dn, preferred_element_type=F32)
        if diagonal:
            qpos = lax.broadcasted_iota(jnp.int32, (tq, tkc), 0) + i * tq
            kpos = lax.broadcasted_iota(jnp.int32, (tq, tkc), 1) + c * tkc
            s = jnp.where(kpos <= qpos, s, NEG_BIG)
        s_ref[...] = s

    def softmax_pv(c, s_ref):
        start = pl.multiple_of(c * tkc, tkc)
        vb = vaug_ref[pl.ds(start, tkc), :]
        s = s_ref[...]
        m_old = m_ref[...]
        m_new = jnp.maximum(m_old, jnp.max(s, axis=-1, keepdims=True))
        alpha = jnp.exp2(m_old - m_new)
        p = jnp.exp2(s - m_new)
        acc_ref[...] = alpha * acc_ref[...] + jnp.dot(p.astype(BF16), vb, preferred_element_type=F32)
        m_ref[...] = m_new

    m_ref[...] = jnp.full(m_ref.shape, NEG_BIG, F32)
    acc_ref[...] = jnp.zeros_like(acc_ref)
    c_own = (i * tq) // tkc
    logits(c_own, s0_ref, True)
    pairs = c_own // 2

    def body(g, carry):
        c = c_own - 2 * g
        logits(c - 1, s1_ref, False)
        softmax_pv(c, s0_ref)
        logits(c - 2, s0_ref, False)
        softmax_pv(c - 1, s1_ref)
        return carry

    lax.fori_loop(0, pairs, body, 0)
    c_tail = c_own - 2 * pairs

    @pl.when(c_tail == 1)
    def _():
        logits(0, s1_ref, False)
        softmax_pv(1, s0_ref)
        softmax_pv(0, s1_ref)

    @pl.when(c_tail == 0)
    def _():
        softmax_pv(0, s0_ref)

    acc = acc_ref[...]
    o_ref[...] = (acc[:, :d] / acc[:, d:]).astype(o_ref.dtype)


def moba_attention(proj, batch, seq, n_heads, q_col, k_col, v_col, tq=512, tkc=1024):
    d = HEAD_DIM
    blk = MOBA_BLOCK
    nb = seq // blk
    assert seq % blk == 0 and nb <= N_PEN_COLS
    tkc = _tile(seq, tkc, blk)
    tq = _tile(tkc, tq, blk)
    nq = seq // tq
    scale = 1.0 / math.sqrt(d)
    slopes = np.exp2(-8.0 * np.arange(1, n_heads + 1, dtype=np.float32) / n_heads).astype(np.float32)
    pieces = _bf16_pieces(slopes * np.float32(LOG2E), 3)
    cst = np.zeros((n_heads, 1, LANES), np.float32)
    for j in range(AUG_BIAS_COLS):
        cst[:, 0, N_PEN_COLS + j] = pieces[j % 3]
    pos = jnp.arange(seq, dtype=jnp.int32)
    onehot = ((pos[:, None] // blk) == jnp.arange(N_PEN_COLS)[None, :]).astype(BF16)
    pos_hi = ((pos // POS_SPLIT) * POS_SPLIT).astype(BF16)
    pos_lo = (pos % POS_SPLIT).astype(BF16)
    aug = jnp.concatenate([
        onehot,
        jnp.stack([pos_hi] * 3 + [pos_lo] * 3, axis=-1),
        jnp.zeros((seq, d - N_PEN_COLS - AUG_BIAS_COLS), BF16)], axis=-1)
    pool = (((jnp.arange(seq)[None, :] // blk) == jnp.arange(LANES)[:, None]).astype(F32) * (1.0 / blk)).astype(BF16)
    return pl.pallas_call(
        functools.partial(_moba_kernel, blk=blk, tq=tq, tkc=tkc, scale=scale),
        out_shape=jax.ShapeDtypeStruct((batch * seq, n_heads * d), BF16),
        grid=(batch, n_heads, nq),
        in_specs=[pl.BlockSpec((tq, d), lambda b, h, i: (b * nq + i, q_col + h)),
                  pl.BlockSpec((seq, d), lambda b, h, i: (b, k_col + h)),
                  pl.BlockSpec((seq, d), lambda b, h, i: (b, v_col + h)),
                  pl.BlockSpec((seq, d), lambda b, h, i: (0, 0)),
                  pl.BlockSpec((1, 1, LANES), lambda b, h, i: (h, 0, 0)),
                  pl.BlockSpec((LANES, seq), lambda b, h, i: (0, 0))],
        out_specs=pl.BlockSpec((tq, d), lambda b, h, i: (b * nq + i, h)),
        scratch_shapes=[pltpu.VMEM((seq, 2 * d), BF16), pltpu.VMEM((seq, 2 * d), BF16),
                        pltpu.VMEM((LANES, d), F32), pltpu.VMEM((tq, 2 * d), F32),
                        pltpu.VMEM((tq, 1), F32), pltpu.VMEM((tq, tkc), F32), pltpu.VMEM((tq, tkc), F32)],
        compiler_params=_cparams(("parallel", "parallel", "arbitrary")),
        name="moba_attention",
    )(proj, proj, proj, aug, jnp.asarray(cst), pool)


def _split(x):
    hi = x.astype(BF16)
    lo = (x - hi.astype(F32)).astype(BF16)
    return hi, lo


def _router_kernel(x_ref, g_ref, w_ref, b_ref, idx_ref, wgt_ref, *, n_experts):
    h = _rms(x_ref[...], g_ref[...])
    h_hi, h_lo = _split(h)
    w_hi, w_lo = _split(w_ref[...])
    logits = (jnp.dot(h_hi, w_hi, preferred_element_type=F32)
              + jnp.dot(h_lo, w_hi, preferred_element_type=F32)
              + jnp.dot(h_hi, w_lo, preferred_element_type=F32)) + b_ref[...]
    tm = logits.shape[0]
    lane = lax.broadcasted_iota(jnp.int32, (tm, LANES), 1)
    logits = jnp.where(lane < n_experts, logits, -jnp.inf)
    idx_out = jnp.zeros((tm, LANES), jnp.int32)
    val_out = jnp.full((tm, LANES), -jnp.inf, F32)
    top = None
    for r in range(TOP_K):
        best = jnp.max(logits, axis=-1, keepdims=True)
        first = jnp.min(jnp.where(logits == best, lane, LANES), axis=-1, keepdims=True)
        if r == 0:
            top = best
        idx_out = jnp.where(lane == r, first, idx_out)
        val_out = jnp.where(lane == r, best, val_out)
        logits = jnp.where(lane == first, -jnp.inf, logits)
    e = jnp.exp(val_out - top)
    idx_ref[...] = idx_out
    wgt_ref[...] = e / jnp.sum(e, axis=-1, keepdims=True)


def router(x, g, w_router, b_router, tm=256):
    n, d = x.shape
    n_experts = w_router.shape[1]
    tm = _tile(n, tm, 8)
    w_pad = jnp.zeros((d, LANES), F32).at[:, :n_experts].set(w_router)
    b_pad = jnp.zeros((1, LANES), F32).at[0, :n_experts].set(b_router)
    idx, wgt = pl.pallas_call(
        functools.partial(_router_kernel, n_experts=n_experts),
        out_shape=(jax.ShapeDtypeStruct((n, LANES), jnp.int32),
                   jax.ShapeDtypeStruct((n, LANES), F32)),
        grid=(n // tm,),
        in_specs=[pl.BlockSpec((tm, d), lambda i: (i, 0)),
                  pl.BlockSpec((1, d), lambda i: (0, 0)),
                  pl.BlockSpec((d, LANES), lambda i: (0, 0)),
                  pl.BlockSpec((1, LANES), lambda i: (0, 0))],
        out_specs=(pl.BlockSpec((tm, LANES), lambda i: (i, 0)),
                   pl.BlockSpec((tm, LANES), lambda i: (i, 0))),
        compiler_params=_cparams(("parallel",)),
        name="router",
    )(x, g.reshape(1, d).astype(F32), w_pad, b_pad)
    return idx[:, :TOP_K], wgt[:, :TOP_K]


def _row_copy(src_hbm, dst_vmem, sem, src_row, dst_row):
    return pltpu.make_async_copy(src_hbm.at[pl.ds(src_row, 1), :],
                                 dst_vmem.at[pl.ds(dst_row, 1), :], sem)


def _gather_kernel(cur_ref, nxt_ref, x_hbm, g_ref, o_ref, buf_ref, sem, *, rows, n_steps):
    s = pl.program_id(0)

    def start_rows(idx_ref, slot):
        def start(r, c):
            _row_copy(x_hbm, buf_ref.at[slot], sem.at[slot], idx_ref[0, 0, r], r).start()
            return c
        lax.fori_loop(0, rows, start, 0, unroll=8)

    def wait_rows(slot):
        def wait(r, c):
            _row_copy(x_hbm, buf_ref.at[slot], sem.at[slot], 0, r).wait()
            return c
        lax.fori_loop(0, rows, wait, 0, unroll=8)

    @pl.when(s == 0)
    def _():
        start_rows(cur_ref, 0)

    for slot in range(2):
        @pl.when(lax.rem(s, 2) == slot)
        def _():
            @pl.when(s + 1 < n_steps)
            def _():
                start_rows(nxt_ref, 1 - slot)
            wait_rows(slot)
            o_ref[...] = _rms(buf_ref[slot], g_ref[...]).astype(o_ref.dtype)


def gather_rows_rmsnorm(x, g, src, rows=256):
    n, d = x.shape
    p = src.shape[0]
    rows = _tile(p, rows, 8)
    nt = p // rows
    src3 = src.reshape(nt, 1, rows)
    return pl.pallas_call(
        functools.partial(_gather_kernel, rows=rows, n_steps=nt),
        out_shape=jax.ShapeDtypeStruct((p, d), BF16),
        grid=(nt,),
        in_specs=[pl.BlockSpec((1, 1, rows), lambda i: (i, 0, 0), memory_space=pltpu.SMEM),
                  pl.BlockSpec((1, 1, rows), lambda i: (jnp.minimum(i + 1, nt - 1), 0, 0), memory_space=pltpu.SMEM),
                  pl.BlockSpec(memory_space=pl.ANY),
                  pl.BlockSpec((1, d), lambda i: (0, 0))],
        out_specs=pl.BlockSpec((rows, d), lambda i: (i, 0)),
        scratch_shapes=[pltpu.VMEM((2, rows, d), F32), pltpu.SemaphoreType.DMA((2,))],
        compiler_params=_cparams(("arbitrary",)),
        name="moe_gather",
    )(src3, src3, x, g.reshape(1, d).astype(F32))


def _new_expert(te_ref, t):
    return jnp.logical_or(t == 0, te_ref[t] != te_ref[jnp.maximum(t - 1, 0)])


def _gate_up_kernel(te_ref, nv_ref, x_ref, wg_ref, wu_ref, bg_ref, bu_ref, o_ref, wgb_ref, wub_ref):
    t = pl.program_id(1)

    @pl.when(t < nv_ref[0])
    def _():
        @pl.when(_new_expert(te_ref, t))
        def _():
            wgb_ref[...] = wg_ref[0].astype(BF16)
            wub_ref[...] = wu_ref[0].astype(BF16)

        x = x_ref[...]
        g = jnp.dot(x, wgb_ref[...], preferred_element_type=F32) + bg_ref[0]
        u = jnp.dot(x, wub_ref[...], preferred_element_type=F32) + bu_ref[0]
        g = jnp.minimum(g, SWIGLU_LIMIT)
        u = jnp.clip(u, -SWIGLU_LIMIT, SWIGLU_LIMIT)
        o_ref[...] = (g * jax.nn.sigmoid(SWIGLU_ALPHA * g) * (u + 1.0)).astype(o_ref.dtype)

    @pl.when(t >= nv_ref[0])
    def _():
        o_ref[...] = jnp.zeros_like(o_ref)


def expert_gate_up(xs, tile_expert, n_valid, w_gate_up, b_gate_up, tm, tf=512):
    p, d = xs.shape
    n_e, _, f2 = w_gate_up.shape
    f = f2 // 2
    tf = _tile(f, tf)
    nf = f // tf
    nt = p // tm
    b3 = b_gate_up.reshape(n_e, 1, f2)

    def row_tile(c, t, te, nv):
        return jnp.minimum(t, nv[0] - 1)

    grid_spec = pltpu.PrefetchScalarGridSpec(
        num_scalar_prefetch=2,
        grid=(nf, nt),
        in_specs=[pl.BlockSpec((tm, d), lambda c, t, te, nv: (row_tile(c, t, te, nv), 0)),
                  pl.BlockSpec((1, d, tf), lambda c, t, te, nv: (te[t], 0, c)),
                  pl.BlockSpec((1, d, tf), lambda c, t, te, nv: (te[t], 0, nf + c)),
                  pl.BlockSpec((1, 1, tf), lambda c, t, te, nv: (te[t], 0, c)),
                  pl.BlockSpec((1, 1, tf), lambda c, t, te, nv: (te[t], 0, nf + c))],
        out_specs=pl.BlockSpec((tm, tf), lambda c, t, te, nv: (t, c)),
        scratch_shapes=[pltpu.VMEM((d, tf), BF16), pltpu.VMEM((d, tf), BF16)],
    )
    return pl.pallas_call(
        _gate_up_kernel,
        out_shape=jax.ShapeDtypeStruct((p, f), BF16),
        grid_spec=grid_spec,
        compiler_params=_cparams(("arbitrary", "arbitrary")),
        name="moe_gate_up",
    )(tile_expert, n_valid, xs, w_gate_up, w_gate_up, b3, b3)


def _down_kernel(te_ref, nv_ref, h_ref, w_ref, b_ref, o_ref, wb_ref):
    t = pl.program_id(1)

    @pl.when(t < nv_ref[0])
    def _():
        @pl.when(_new_expert(te_ref, t))
        def _():
            wb_ref[...] = w_ref[0].astype(BF16)

        y = jnp.dot(h_ref[...], wb_ref[...], preferred_element_type=F32) + b_ref[0]
        o_ref[...] = y.astype(o_ref.dtype)

    @pl.when(t >= nv_ref[0])
    def _():
        o_ref[...] = jnp.zeros_like(o_ref)


def expert_down(hid, tile_expert, n_valid, w_down, b_down, tm, tn=1024):
    p, f = hid.shape
    n_e, _, d = w_down.shape
    tn = _tile(d, tn)
    nd = d // tn
    nt = p // tm
    b3 = b_down.reshape(n_e, 1, d)

    def row_tile(c, t, te, nv):
        return jnp.minimum(t, nv[0] - 1)

    grid_spec = pltpu.PrefetchScalarGridSpec(
        num_scalar_prefetch=2,
        grid=(nd, nt),
        in_specs=[pl.BlockSpec((tm, f), lambda c, t, te, nv: (row_tile(c, t, te, nv), 0)),
                  pl.BlockSpec((1, f, tn), lambda c, t, te, nv: (te[t], 0, c)),
                  pl.BlockSpec((1, 1, tn), lambda c, t, te, nv: (te[t], 0, c))],
        out_specs=pl.BlockSpec((tm, tn), lambda c, t, te, nv: (t, c)),
        scratch_shapes=[pltpu.VMEM((f, tn), BF16)],
    )
    return pl.pallas_call(
        _down_kernel,
        out_shape=jax.ShapeDtypeStruct((p, d), F32),
        grid_spec=grid_spec,
        compiler_params=_cparams(("arbitrary", "arbitrary")),
        name="moe_down",
    )(tile_expert, n_valid, hid, w_down, b3)


def _combine_kernel(cur_ref, nxt_ref, x_ref, w_ref, y_hbm, g_ref, o_ref, buf_ref, sem, *, tt, n_steps):
    s = pl.program_id(0)

    def start_rows(idx_ref, slot):
        def start(r, c):
            for k in range(TOP_K):
                _row_copy(y_hbm, buf_ref.at[slot, k], sem.at[slot], idx_ref[0, 0, r * TOP_K + k], r).start()
            return c
        lax.fori_loop(0, tt, start, 0, unroll=2)

    def wait_rows(slot):
        def wait(r, c):
            for k in range(TOP_K):
                _row_copy(y_hbm, buf_ref.at[slot, k], sem.at[slot], 0, r).wait()
            return c
        lax.fori_loop(0, tt, wait, 0, unroll=2)

    @pl.when(s == 0)
    def _():
        start_rows(cur_ref, 0)

    for slot in range(2):
        @pl.when(lax.rem(s, 2) == slot)
        def _():
            @pl.when(s + 1 < n_steps)
            def _():
                start_rows(nxt_ref, 1 - slot)
            wait_rows(slot)
            acc = x_ref[...]
            for k in range(TOP_K):
                acc = acc + w_ref[k] * buf_ref[slot, k]
            o_ref[...] = _rms(acc, g_ref[...]).astype(o_ref.dtype)


def combine(x, y, pos, top_w, g, tt=128):
    n, d = x.shape
    tt = _tile(n, tt, 8)
    nt = n // tt
    pos3 = pos.reshape(nt, 1, tt * TOP_K)
    return pl.pallas_call(
        functools.partial(_combine_kernel, tt=tt, n_steps=nt),
        out_shape=jax.ShapeDtypeStruct((n, d), F32),
        grid=(nt,),
        in_specs=[pl.BlockSpec((1, 1, tt * TOP_K), lambda i: (i, 0, 0), memory_space=pltpu.SMEM),
                  pl.BlockSpec((1, 1, tt * TOP_K), lambda i: (jnp.minimum(i + 1, nt - 1), 0, 0),
                               memory_space=pltpu.SMEM),
                  pl.BlockSpec((tt, d), lambda i: (i, 0)),
                  pl.BlockSpec((TOP_K, tt, 1), lambda i: (0, i, 0)),
                  pl.BlockSpec(memory_space=pl.ANY),
                  pl.BlockSpec((1, d), lambda i: (0, 0))],
        out_specs=pl.BlockSpec((tt, d), lambda i: (i, 0)),
        scratch_shapes=[pltpu.VMEM((2, TOP_K, tt, d), F32), pltpu.SemaphoreType.DMA((2,))],
        compiler_params=_cparams(("arbitrary",)),
        name="moe_combine",
    )(pos3, pos3, x, top_w.T.reshape(TOP_K, n, 1), y, g.reshape(1, d).astype(F32))


def route_plan(top_idx, n_experts, tm):
    n = top_idx.shape[0]
    pairs = n * TOP_K
    p_rows = pairs + n_experts * tm
    e_flat = top_idx.reshape(pairs)
    onehot = (e_flat[:, None] == jnp.arange(n_experts)[None, :]).astype(jnp.int32)
    csum = jnp.cumsum(onehot, axis=0)
    rank = jnp.take_along_axis(csum, e_flat[:, None], axis=1)[:, 0] - 1
    counts = csum[-1]
    padded = ((counts + tm - 1) // tm) * tm
    ends = jnp.cumsum(padded)
    starts = ends - padded
    pos = starts[e_flat] + rank
    token = jnp.arange(pairs, dtype=jnp.int32) // TOP_K
    src = jnp.zeros((p_rows,), jnp.int32).at[pos].set(token)
    n_tiles = p_rows // tm
    tile_start = jnp.arange(n_tiles, dtype=jnp.int32) * tm
    tile_expert = jnp.minimum(jnp.sum((ends[None, :] <= tile_start[:, None]).astype(jnp.int32), axis=1), n_experts - 1)
    n_valid = (ends[-1] // tm).astype(jnp.int32).reshape(1)
    tile_expert = jnp.where(tile_start < ends[-1], tile_expert, tile_expert[jnp.maximum(n_valid[0] - 1, 0)])
    return pos.astype(jnp.int32), src, tile_expert.astype(jnp.int32), n_valid


def moe_block(x1, norm_ffn, w_router, b_router, w_gate_up, b_gate_up, w_down, b_down, norm_final, tm=512):
    n_experts = w_router.shape[1]
    top_idx, top_w = router(x1, norm_ffn, w_router, b_router)
    pos, src, tile_expert, n_valid = route_plan(top_idx, n_experts, tm)
    xs = gather_rows_rmsnorm(x1, norm_ffn, src)
    hid = expert_gate_up(xs, tile_expert, n_valid, w_gate_up, b_gate_up, tm)
    y = expert_down(hid, tile_expert, n_valid, w_down, b_down, tm)
    return combine(x1, y, pos, top_w, norm_final)


def kernel(x, norm_mix, w_in, w_up_sb, w_up_moba, w_branch_gate, b_branch_gate, w_out, norm_ffn,
           w_router, b_router, w_gate_up, b_gate_up, w_down, b_down, norm_final):
    b, t, d = x.shape
    n = b * t
    assert norm_mix.shape[0] == 1, "single-layer block"
    x2 = x.reshape(n, d)
    c_sb = w_up_sb.shape[1] // HEAD_DIM
    c_mb = w_up_moba.shape[1] // HEAD_DIM
    h = rmsnorm(x2, norm_mix[0], BF16)
    proj = matmul(h, w_in[0].astype(BF16), BF16)
    gates = matmul(h, w_branch_gate[0].astype(BF16), BF16, "sigmoid_bias", b_branch_gate[0])
    o_sb = sb_attention(proj, b, t, c_sb, 0, c_sb, 2 * c_sb)
    o_mb = moba_attention(proj, b, t, c_mb, 3 * c_sb, 3 * c_sb + c_mb, 3 * c_sb + 2 * c_mb)
    mixed = branch_mix(o_sb, o_mb, w_up_sb[0].astype(BF16), w_up_moba[0].astype(BF16), gates)
    x1 = matmul(mixed, w_out[0].astype(BF16), F32, "residual", x2)
    out = moe_block(x1, norm_ffn[0], w_router[0], b_router[0], w_gate_up[0], b_gate_up[0],
                    w_down[0], b_down[0], norm_final)
    return out.reshape(b, t, d)
```

```python
import functools
import math

import jax
import jax.numpy as jnp
import numpy as np
from jax import lax
from jax.experimental import pallas as pl
from jax.experimental.pallas import tpu as pltpu

F32 = jnp.float32
BF16 = jnp.bfloat16

HEAD_DIM = 128
MOBA_BLOCK = 256
MOBA_TOPK = 3
TOP_K = 4
SWIGLU_ALPHA = 1.702
SWIGLU_LIMIT = 7.0
RMS_EPS = 1e-5
LOG2E = 1.4426950408889634
LANES = 128
NEG_BIG = -1e30
VMEM_LIMIT = 56 * 1024 * 1024
N_PEN_COLS = 32
AUG_BIAS_COLS = 6
SB_DONE_BITS = 160.0
POS_SPLIT = 64


def _bf16_pieces(x, n):
    out, r = [], np.asarray(x, np.float32)
    for _ in range(n):
        piece = (r.view(np.uint32) & np.uint32(0xFFFF0000)).view(np.float32)
        out.append(piece)
        r = (r - piece).astype(np.float32)
    return out


def _cparams(sem):
    return pltpu.CompilerParams(dimension_semantics=sem, vmem_limit_bytes=VMEM_LIMIT)


def _tile(dim, target, align=LANES):
    if dim <= target:
        return dim
    t = (target // align) * align
    while t > align and dim % t:
        t -= align
    assert dim % t == 0, (dim, target)
    return t


def _rms(x, g):
    ms = jnp.mean(x * x, axis=-1, keepdims=True)
    return x * lax.rsqrt(ms + RMS_EPS) * g


def _rmsnorm_kernel(x_ref, g_ref, o_ref):
    o_ref[...] = _rms(x_ref[...].astype(F32), g_ref[...]).astype(o_ref.dtype)


def rmsnorm(x, g, out_dtype, tm=256):
    n, d = x.shape
    tm = _tile(n, tm, 8)
    return pl.pallas_call(
        _rmsnorm_kernel,
        out_shape=jax.ShapeDtypeStruct((n, d), out_dtype),
        grid=(n // tm,),
        in_specs=[pl.BlockSpec((tm, d), lambda i: (i, 0)),
                  pl.BlockSpec((1, d), lambda i: (0, 0))],
        out_specs=pl.BlockSpec((tm, d), lambda i: (i, 0)),
        compiler_params=_cparams(("parallel",)),
        name="rmsnorm",
    )(x, g.reshape(1, d).astype(F32))


def _mm_kernel(*refs, epilogue, nk):
    if epilogue == "none":
        a_ref, b_ref, o_ref, acc_ref = refs
        e_ref = None
    else:
        a_ref, b_ref, e_ref, o_ref, acc_ref = refs
    k = pl.program_id(2)

    @pl.when(k == 0)
    def _():
        acc_ref[...] = jnp.zeros_like(acc_ref)

    acc_ref[...] += jnp.dot(a_ref[...], b_ref[...], preferred_element_type=F32)

    @pl.when(k == nk - 1)
    def _():
        acc = acc_ref[...]
        if epilogue == "sigmoid_bias":
            acc = jax.nn.sigmoid(acc + e_ref[...])
        elif epilogue == "residual":
            acc = acc + e_ref[...]
        o_ref[...] = acc.astype(o_ref.dtype)


def matmul(a, b, out_dtype, epilogue="none", extra=None, tm=1024, tn=1024, tk=1024):
    m, kdim = a.shape
    _, n = b.shape
    tm, tn, tk = _tile(m, tm), _tile(n, tn), _tile(kdim, tk)
    nk = kdim // tk
    in_specs = [pl.BlockSpec((tm, tk), lambda i, j, k: (i, k)),
                pl.BlockSpec((tk, tn), lambda i, j, k: (k, j))]
    args = [a, b]
    if epilogue == "sigmoid_bias":
        in_specs.append(pl.BlockSpec((1, tn), lambda i, j, k: (0, j)))
        args.append(extra.reshape(1, n).astype(F32))
    elif epilogue == "residual":
        in_specs.append(pl.BlockSpec((tm, tn), lambda i, j, k: (i, j)))
        args.append(extra)
    return pl.pallas_call(
        functools.partial(_mm_kernel, epilogue=epilogue, nk=nk),
        out_shape=jax.ShapeDtypeStruct((m, n), out_dtype),
        grid=(m // tm, n // tn, nk),
        in_specs=in_specs,
        out_specs=pl.BlockSpec((tm, tn), lambda i, j, k: (i, j)),
        scratch_shapes=[pltpu.VMEM((tm, tn), F32)],
        compiler_params=_cparams(("parallel", "parallel", "arbitrary")),
        name="matmul_" + epilogue,
    )(*args)


def _mix_kernel(osb_ref, omb_ref, wsb_ref, wmb_ref, g0_ref, g1_ref, o_ref):
    u_sb = jnp.dot(osb_ref[...], wsb_ref[...], preferred_element_type=F32)
    u_mb = jnp.dot(omb_ref[...], wmb_ref[...], preferred_element_type=F32)
    mixed = g0_ref[...].astype(F32) * u_sb + g1_ref[...].astype(F32) * u_mb
    o_ref[...] = mixed.astype(o_ref.dtype)


def branch_mix(o_sb, o_mb, w_sb, w_mb, gates, tm=512, tn=1024):
    m, kdim = o_sb.shape
    d = w_sb.shape[1]
    tm, tn = _tile(m, tm), _tile(d, tn)
    nj = d // tn
    return pl.pallas_call(
        _mix_kernel,
        out_shape=jax.ShapeDtypeStruct((m, d), BF16),
        grid=(m // tm, nj),
        in_specs=[pl.BlockSpec((tm, kdim), lambda i, j: (i, 0)),
                  pl.BlockSpec((tm, kdim), lambda i, j: (i, 0)),
                  pl.BlockSpec((kdim, tn), lambda i, j: (0, j)),
                  pl.BlockSpec((kdim, tn), lambda i, j: (0, j)),
                  pl.BlockSpec((tm, tn), lambda i, j: (i, j)),
                  pl.BlockSpec((tm, tn), lambda i, j: (i, nj + j))],
        out_specs=pl.BlockSpec((tm, tn), lambda i, j: (i, j)),
        compiler_params=_cparams(("parallel", "parallel")),
        name="branch_mix",
    )(o_sb, o_mb, w_sb, w_mb, gates, gates)


def _softplus2(z):
    bits = lax.bitcast_convert_type(z, jnp.uint32) | jnp.uint32(0x80000000)
    neg_abs = lax.bitcast_convert_type(bits, F32)
    return jnp.maximum(z, 0.0) + jnp.log2(1.0 + jnp.exp2(neg_abs))


def _sb_kernel(q_ref, k_ref, v_ref, u_ref, o_ref, acc_ref, run_ref, *, tq, tk, nblk, scale):
    i = pl.program_id(2)
    n_sub = tq // tk
    q = (q_ref[...].astype(F32) * (scale * LOG2E)).astype(BF16)
    dn = (((1,), (1,)), ((), ()))

    def group(j_hi, n, masked):
        lo = j_hi - (n - 1)
        start = pl.multiple_of(lo * tk, tk)
        kb = k_ref[pl.ds(start, n * tk), :]
        vb = v_ref[pl.ds(start, n * tk), :]
        z = lax.dot_general(q, kb, dn, preferred_element_type=F32)
        sp = _softplus2(z)
        if masked:
            row = lax.broadcasted_iota(jnp.int32, (tq, n * tk), 0) + i * tq
            col = lax.broadcasted_iota(jnp.int32, (tq, n * tk), 1) + lo * tk
            past = col < row
            sp = jnp.where(past, sp, 0.0)
        spb = sp.astype(BF16)
        run = run_ref[...]
        parts = [None] * n
        for b in range(n - 1, -1, -1):
            c = jnp.dot(spb[:, b * tk:(b + 1) * tk], u_ref[...], preferred_element_type=F32)
            later = jnp.concatenate([run] * (tk // LANES), axis=1)
            parts[b] = c + later
            run = run + jnp.broadcast_to(c[:, 0:1], run.shape)
        w = jnp.exp2(z - jnp.concatenate(parts, axis=1))
        if masked:
            w = jnp.where(past, w, 0.0)
        acc_ref[...] += jnp.dot(w.astype(BF16), vb, preferred_element_type=F32)
        run_ref[...] = run

    run_ref[...] = jnp.zeros_like(run_ref)
    acc_ref[...] = jnp.zeros_like(acc_ref)

    def diag_body(g, carry):
        group((i + 1) * n_sub - 1 - g * nblk, nblk, True)
        return carry

    lax.fori_loop(0, n_sub // nblk, diag_body, 0)

    n_groups = (i * n_sub) // nblk

    def cond(state):
        g, min_run = state
        return jnp.logical_and(g < n_groups, min_run < SB_DONE_BITS)

    def body(state):
        g, _ = state
        group(i * n_sub - 1 - g * nblk, nblk, False)
        return g + 1, jnp.min(run_ref[...])

    lax.while_loop(cond, body, (jnp.int32(0), jnp.min(run_ref[...])))
    o_ref[...] = acc_ref[...].astype(o_ref.dtype)


def sb_attention(proj, batch, seq, n_heads, q_col, k_col, v_col, tq=1024, tk=256, nblk=2):
    d = HEAD_DIM
    tq = _tile(seq, tq)
    tk = _tile(tq, tk)
    nq = seq // tq
    assert (tq // tk) % nblk == 0
    scale = 1.0 / math.sqrt(d)
    r = jnp.arange(tk)
    u = (r[:, None] >= r[None, :]).astype(BF16)
    return pl.pallas_call(
        functools.partial(_sb_kernel, tq=tq, tk=tk, nblk=nblk, scale=scale),
        out_shape=jax.ShapeDtypeStruct((batch * seq, n_heads * d), BF16),
        grid=(batch, n_heads, nq),
        in_specs=[pl.BlockSpec((tq, d), lambda b, h, i: (b * nq + i, q_col + h)),
                  pl.BlockSpec((seq, d), lambda b, h, i: (b, k_col + h)),
                  pl.BlockSpec((seq, d), lambda b, h, i: (b, v_col + h)),
                  pl.BlockSpec((tk, tk), lambda b, h, i: (0, 0))],
        out_specs=pl.BlockSpec((tq, d), lambda b, h, i: (b * nq + i, h)),
        scratch_shapes=[pltpu.VMEM((tq, d), F32), pltpu.VMEM((tq, LANES), F32)],
        compiler_params=_cparams(("parallel", "parallel", "arbitrary")),
        name="sb_attention",
    )(proj, proj, proj, u)


def _moba_kernel(q_ref, k_ref, v_ref, aug_ref, cst_ref, pool_ref, o_ref,
                 kaug_ref, vaug_ref, kmean_ref, acc_ref, m_ref, s0_ref, s1_ref, *, blk, tq, tkc, scale):
    i = pl.program_id(2)
    d = HEAD_DIM

    @pl.when(i == 0)
    def _():
        kaug_ref[:, :d] = k_ref[...]
        kaug_ref[:, d:] = aug_ref[...]
        vaug_ref[:, :d] = v_ref[...]
        vaug_ref[:, d:] = jnp.ones((vaug_ref.shape[0], d), BF16)
        kmean_ref[...] = jnp.dot(pool_ref[...], k_ref[...], preferred_element_type=F32)

    q_raw = q_ref[...]
    dn = (((1,), (1,)), ((), ()))

    km = kmean_ref[...]
    km_hi = km.astype(BF16)
    km_lo = (km - km_hi.astype(F32)).astype(BF16)
    gate = (lax.dot_general(q_raw, km_hi, dn, preferred_element_type=F32)
            + lax.dot_general(q_raw, km_lo, dn, preferred_element_type=F32))
    lane = lax.broadcasted_iota(jnp.int32, (tq, LANES), 1)
    row_blk = lax.shift_right_logical(lax.broadcasted_iota(jnp.int32, (tq, LANES), 0), int(math.log2(blk)))
    own = i * (tq // blk) + row_blk
    is_past = lane < own
    gate = jnp.where(is_past, gate, -jnp.inf)
    pen = jnp.where(is_past, NEG_BIG, 0.0)
    for r in range(MOBA_TOPK):
        best = jnp.max(gate, axis=-1, keepdims=True)
        first = jnp.min(jnp.where(gate == best, lane, LANES), axis=-1, keepdims=True)
        hit = (lane == first) & is_past
        pen = jnp.where(hit, 0.0, pen)
        gate = jnp.where(lane == first, -jnp.inf, gate)

    qs = (q_raw.astype(F32) * (scale * LOG2E)).astype(BF16)
    aux = jnp.where(lane < N_PEN_COLS, pen, cst_ref[0])
    q_aug = jnp.concatenate([qs, aux.astype(BF16)], axis=1)

    def logits(c, s_ref, diagonal):
        start = pl.multiple_of(c * tkc, tkc)
        kb = kaug_ref[pl.ds(start, tkc), :]
        s = lax.dot_general(q_aug, kb, dn, preferred_element_type=F32)
        if diagonal:
            qpos = lax.broadcasted_iota(jnp.int32, (tq, tkc), 0) + i * tq
            kpos = lax.broadcasted_iota(jnp.int32, (tq, tkc), 1) + c * tkc
            s = jnp.where(kpos <= qpos, s, NEG_BIG)
        s_ref[...] = s

    def softmax_pv(c, s_ref):
        start = pl.multiple_of(c * tkc, tkc)
        vb = vaug_ref[pl.ds(start, tkc), :]
        s = s_ref[...]
        m_old = m_ref[...]
        m_new = jnp.maximum(m_old, jnp.max(s, axis=-1, keepdims=True))
        alpha = jnp.exp2(m_old - m_new)
        p = jnp.exp2(s - m_new)
        acc_ref[...] = alpha * acc_ref[...] + jnp.dot(p.astype(BF16), vb, preferred_element_type=F32)
        m_ref[...] = m_new

    m_ref[...] = jnp.full(m_ref.shape, NEG_BIG, F32)
    acc_ref[...] = jnp.zeros_like(acc_ref)
    c_own = (i * tq) // tkc
    logits(c_own, s0_ref, True)
    pairs = c_own // 2

    def body(g, carry):
        c = c_own - 2 * g
        logits(c - 1, s1_ref, False)
        softmax_pv(c, s0_ref)
        logits(c - 2, s0_ref, False)
        softmax_pv(c - 1, s1_ref)
        return carry

    lax.fori_loop(0, pairs, body, 0)
    c_tail = c_own - 2 * pairs

    @pl.when(c_tail == 1)
    def _():
        logits(0, s1_ref, False)
        softmax_pv(1, s0_ref)
        softmax_pv(0, s1_ref)

    @pl.when(c_tail == 0)
    def _():
        softmax_pv(0, s0_ref)

    acc = acc_ref[...]
    o_ref[...] = (acc[:, :d] / acc[:, d:]).astype(o_ref.dtype)


def moba_attention(proj, batch, seq, n_heads, q_col, k_col, v_col, tq=1024, tkc=1024):
    d = HEAD_DIM
    blk = MOBA_BLOCK
    nb = seq // blk
    assert seq % blk == 0 and nb <= N_PEN_COLS
    tkc = _tile(seq, tkc, blk)
    tq = _tile(tkc, tq, blk)
    nq = seq // tq
    scale = 1.0 / math.sqrt(d)
    slopes = np.exp2(-8.0 * np.arange(1, n_heads + 1, dtype=np.float32) / n_heads).astype(np.float32)
    pieces = _bf16_pieces(slopes * np.float32(LOG2E), 3)
    cst = np.zeros((n_heads, 1, LANES), np.float32)
    for j in range(AUG_BIAS_COLS):
        cst[:, 0, N_PEN_COLS + j] = pieces[j % 3]
    pos = jnp.arange(seq, dtype=jnp.int32)
    onehot = ((pos[:, None] // blk) == jnp.arange(N_PEN_COLS)[None, :]).astype(BF16)
    pos_hi = ((pos // POS_SPLIT) * POS_SPLIT).astype(BF16)
    pos_lo = (pos % POS_SPLIT).astype(BF16)
    aug = jnp.concatenate([
        onehot,
        jnp.stack([pos_hi] * 3 + [pos_lo] * 3, axis=-1),
        jnp.zeros((seq, d - N_PEN_COLS - AUG_BIAS_COLS), BF16)], axis=-1)
    pool = (((jnp.arange(seq)[None, :] // blk) == jnp.arange(LANES)[:, None]).astype(F32) * (1.0 / blk)).astype(BF16)
    return pl.pallas_call(
        functools.partial(_moba_kernel, blk=blk, tq=tq, tkc=tkc, scale=scale),
        out_shape=jax.ShapeDtypeStruct((batch * seq, n_heads * d), BF16),
        grid=(batch, n_heads, nq),
        in_specs=[pl.BlockSpec((tq, d), lambda b, h, i: (b * nq + i, q_col + h)),
                  pl.BlockSpec((seq, d), lambda b, h, i: (b, k_col + h)),
                  pl.BlockSpec((seq, d), lambda b, h, i: (b, v_col + h)),
                  pl.BlockSpec((seq, d), lambda b, h, i: (0, 0)),
                  pl.BlockSpec((1, 1, LANES), lambda b, h, i: (h, 0, 0)),
                  pl.BlockSpec((LANES, seq), lambda b, h, i: (0, 0))],
        out_specs=pl.BlockSpec((tq, d), lambda b, h, i: (b * nq + i, h)),
        scratch_shapes=[pltpu.VMEM((seq, 2 * d), BF16), pltpu.VMEM((seq, 2 * d), BF16),
                        pltpu.VMEM((LANES, d), F32), pltpu.VMEM((tq, 2 * d), F32),
                        pltpu.VMEM((tq, 1), F32), pltpu.VMEM((tq, tkc), F32), pltpu.VMEM((tq, tkc), F32)],
        compiler_params=_cparams(("parallel", "parallel", "arbitrary")),
        name="moba_attention",
    )(proj, proj, proj, aug, jnp.asarray(cst), pool)


def _split(x):
    hi = x.astype(BF16)
    lo = (x - hi.astype(F32)).astype(BF16)
    return hi, lo


def _router_kernel(x_ref, g_ref, w_ref, b_ref, idx_ref, wgt_ref, *, n_experts):
    h = _rms(x_ref[...], g_ref[...])
    h_hi, h_lo = _split(h)
    w_hi, w_lo = _split(w_ref[...])
    logits = (jnp.dot(h_hi, w_hi, preferred_element_type=F32)
              + jnp.dot(h_lo, w_hi, preferred_element_type=F32)
              + jnp.dot(h_hi, w_lo, preferred_element_type=F32)) + b_ref[...]
    tm = logits.shape[0]
    lane = lax.broadcasted_iota(jnp.int32, (tm, LANES), 1)
    logits = jnp.where(lane < n_experts, logits, -jnp.inf)
    idx_out = jnp.zeros((tm, LANES), jnp.int32)
    val_out = jnp.full((tm, LANES), -jnp.inf, F32)
    top = None
    for r in range(TOP_K):
        best = jnp.max(logits, axis=-1, keepdims=True)
        first = jnp.min(jnp.where(logits == best, lane, LANES), axis=-1, keepdims=True)
        if r == 0:
            top = best
        idx_out = jnp.where(lane == r, first, idx_out)
        val_out = jnp.where(lane == r, best, val_out)
        logits = jnp.where(lane == first, -jnp.inf, logits)
    e = jnp.exp(val_out - top)
    idx_ref[...] = idx_out
    wgt_ref[...] = e / jnp.sum(e, axis=-1, keepdims=True)


def router(x, g, w_router, b_router, tm=256):
    n, d = x.shape
    n_experts = w_router.shape[1]
    tm = _tile(n, tm, 8)
    w_pad = jnp.zeros((d, LANES), F32).at[:, :n_experts].set(w_router)
    b_pad = jnp.zeros((1, LANES), F32).at[0, :n_experts].set(b_router)
    idx, wgt = pl.pallas_call(
        functools.partial(_router_kernel, n_experts=n_experts),
        out_shape=(jax.ShapeDtypeStruct((n, LANES), jnp.int32),
                   jax.ShapeDtypeStruct((n, LANES), F32)),
        grid=(n // tm,),
        in_specs=[pl.BlockSpec((tm, d), lambda i: (i, 0)),
                  pl.BlockSpec((1, d), lambda i: (0, 0)),
                  pl.BlockSpec((d, LANES), lambda i: (0, 0)),
                  pl.BlockSpec((1, LANES), lambda i: (0, 0))],
        out_specs=(pl.BlockSpec((tm, LANES), lambda i: (i, 0)),
                   pl.BlockSpec((tm, LANES), lambda i: (i, 0))),
        compiler_params=_cparams(("parallel",)),
        name="router",
    )(x, g.reshape(1, d).astype(F32), w_pad, b_pad)
    return idx[:, :TOP_K], wgt[:, :TOP_K]


def _row_copy(src_hbm, dst_vmem, sem, src_row, dst_row):
    return pltpu.make_async_copy(src_hbm.at[pl.ds(src_row, 1), :],
                                 dst_vmem.at[pl.ds(dst_row, 1), :], sem)


def _gather_kernel(cur_ref, nxt_ref, x_hbm, g_ref, o_ref, buf_ref, sem, *, rows, n_steps):
    s = pl.program_id(0)

    def start_rows(idx_ref, slot):
        def start(r8, c):
            for u in range(8):
                r = r8 * 8 + u
                _row_copy(x_hbm, buf_ref.at[slot], sem.at[slot], idx_ref[0, 0, r], r).start(priority=u % 2)
            return c
        lax.fori_loop(0, rows // 8, start, 0)

    def wait_rows(slot):
        def wait(r, c):
            _row_copy(x_hbm, buf_ref.at[slot], sem.at[slot], 0, r).wait()
            return c
        lax.fori_loop(0, rows, wait, 0, unroll=8)

    @pl.when(s == 0)
    def _():
        start_rows(cur_ref, 0)

    for slot in range(2):
        @pl.when(lax.rem(s, 2) == slot)
        def _():
            @pl.when(s + 1 < n_steps)
            def _():
                start_rows(nxt_ref, 1 - slot)
            wait_rows(slot)
            o_ref[...] = _rms(buf_ref[slot], g_ref[...]).astype(o_ref.dtype)


def gather_rows_rmsnorm(x, g, src, rows=256):
    n, d = x.shape
    p = src.shape[0]
    rows = _tile(p, rows, 8)
    nt = p // rows
    src3 = src.reshape(nt, 1, rows)
    return pl.pallas_call(
        functools.partial(_gather_kernel, rows=rows, n_steps=nt),
        out_shape=jax.ShapeDtypeStruct((p, d), BF16),
        grid=(nt,),
        in_specs=[pl.BlockSpec((1, 1, rows), lambda i: (i, 0, 0), memory_space=pltpu.SMEM),
                  pl.BlockSpec((1, 1, rows), lambda i: (jnp.minimum(i + 1, nt - 1), 0, 0), memory_space=pltpu.SMEM),
                  pl.BlockSpec(memory_space=pl.ANY),
                  pl.BlockSpec((1, d), lambda i: (0, 0))],
        out_specs=pl.BlockSpec((rows, d), lambda i: (i, 0)),
        scratch_shapes=[pltpu.VMEM((2, rows, d), F32), pltpu.SemaphoreType.DMA((2,))],
        compiler_params=_cparams(("arbitrary",)),
        name="moe_gather",
    )(src3, src3, x, g.reshape(1, d).astype(F32))


def _new_expert(te_ref, t):
    return jnp.logical_or(t == 0, te_ref[t] != te_ref[jnp.maximum(t - 1, 0)])


def _gate_up_kernel(te_ref, nv_ref, x_ref, wg_ref, wu_ref, bg_ref, bu_ref, o_ref, wgb_ref, wub_ref):
    t = pl.program_id(1)

    @pl.when(t < nv_ref[0])
    def _():
        @pl.when(_new_expert(te_ref, t))
        def _():
            wgb_ref[...] = wg_ref[0].astype(BF16)
            wub_ref[...] = wu_ref[0].astype(BF16)

        x = x_ref[...]
        g = jnp.dot(x, wgb_ref[...], preferred_element_type=F32) + bg_ref[0]
        u = jnp.dot(x, wub_ref[...], preferred_element_type=F32) + bu_ref[0]
        g = jnp.minimum(g, SWIGLU_LIMIT)
        u = jnp.clip(u, -SWIGLU_LIMIT, SWIGLU_LIMIT)
        o_ref[...] = (g * jax.nn.sigmoid(SWIGLU_ALPHA * g) * (u + 1.0)).astype(o_ref.dtype)

    @pl.when(t >= nv_ref[0])
    def _():
        o_ref[...] = jnp.zeros_like(o_ref)


def expert_gate_up(xs, tile_expert, n_valid, w_gate_up, b_gate_up, tm, tf=512):
    p, d = xs.shape
    n_e, _, f2 = w_gate_up.shape
    f = f2 // 2
    tf = _tile(f, tf)
    nf = f // tf
    nt = p // tm
    b3 = b_gate_up.reshape(n_e, 1, f2)

    def row_tile(c, t, te, nv):
        return jnp.minimum(t, nv[0] - 1)

    grid_spec = pltpu.PrefetchScalarGridSpec(
        num_scalar_prefetch=2,
        grid=(nf, nt),
        in_specs=[pl.BlockSpec((tm, d), lambda c, t, te, nv: (row_tile(c, t, te, nv), 0)),
                  pl.BlockSpec((1, d, tf), lambda c, t, te, nv: (te[t], 0, c)),
                  pl.BlockSpec((1, d, tf), lambda c, t, te, nv: (te[t], 0, nf + c)),
                  pl.BlockSpec((1, 1, tf), lambda c, t, te, nv: (te[t], 0, c)),
                  pl.BlockSpec((1, 1, tf), lambda c, t, te, nv: (te[t], 0, nf + c))],
        out_specs=pl.BlockSpec((tm, tf), lambda c, t, te, nv: (t, c)),
        scratch_shapes=[pltpu.VMEM((d, tf), BF16), pltpu.VMEM((d, tf), BF16)],
    )
    return pl.pallas_call(
        _gate_up_kernel,
        out_shape=jax.ShapeDtypeStruct((p, f), BF16),
        grid_spec=grid_spec,
        compiler_params=_cparams(("arbitrary", "arbitrary")),
        name="moe_gate_up",
    )(tile_expert, n_valid, xs, w_gate_up, w_gate_up, b3, b3)


def _down_kernel(te_ref, nv_ref, h_ref, w_ref, b_ref, o_ref, wb_ref):
    t = pl.program_id(1)

    @pl.when(t < nv_ref[0])
    def _():
        @pl.when(_new_expert(te_ref, t))
        def _():
            wb_ref[...] = w_ref[0].astype(BF16)

        y = jnp.dot(h_ref[...], wb_ref[...], preferred_element_type=F32) + b_ref[0]
        o_ref[...] = y.astype(o_ref.dtype)

    @pl.when(t >= nv_ref[0])
    def _():
        o_ref[...] = jnp.zeros_like(o_ref)


def expert_down(hid, tile_expert, n_valid, w_down, b_down, tm, tn=2048):
    p, f = hid.shape
    n_e, _, d = w_down.shape
    tn = _tile(d, tn)
    nd = d // tn
    nt = p // tm
    b3 = b_down.reshape(n_e, 1, d)

    def row_tile(c, t, te, nv):
        return jnp.minimum(t, nv[0] - 1)

    grid_spec = pltpu.PrefetchScalarGridSpec(
        num_scalar_prefetch=2,
        grid=(nd, nt),
        in_specs=[pl.BlockSpec((tm, f), lambda c, t, te, nv: (row_tile(c, t, te, nv), 0)),
                  pl.BlockSpec((1, f, tn), lambda c, t, te, nv: (te[t], 0, c)),
                  pl.BlockSpec((1, 1, tn), lambda c, t, te, nv: (te[t], 0, c))],
        out_specs=pl.BlockSpec((tm, tn), lambda c, t, te, nv: (t, c)),
        scratch_shapes=[pltpu.VMEM((f, tn), BF16)],
    )
    return pl.pallas_call(
        _down_kernel,
        out_shape=jax.ShapeDtypeStruct((p, d), F32),
        grid_spec=grid_spec,
        compiler_params=_cparams(("arbitrary", "arbitrary")),
        name="moe_down",
    )(tile_expert, n_valid, hid, w_down, b3)


def _combine_kernel(cur_ref, nxt_ref, x_ref, w_ref, y_hbm, g_ref, o_ref, buf_ref, sem, *, tt, n_steps):
    s = pl.program_id(0)

    def start_rows(idx_ref, slot):
        def start(r, c):
            for k in range(TOP_K):
                _row_copy(y_hbm, buf_ref.at[slot, k], sem.at[slot], idx_ref[0, 0, r * TOP_K + k], r).start()
            return c
        lax.fori_loop(0, tt, start, 0, unroll=2)

    def wait_rows(slot):
        def wait(r, c):
            for k in range(TOP_K):
                _row_copy(y_hbm, buf_ref.at[slot, k], sem.at[slot], 0, r).wait()
            return c
        lax.fori_loop(0, tt, wait, 0, unroll=2)

    @pl.when(s == 0)
    def _():
        start_rows(cur_ref, 0)

    for slot in range(2):
        @pl.when(lax.rem(s, 2) == slot)
        def _():
            @pl.when(s + 1 < n_steps)
            def _():
                start_rows(nxt_ref, 1 - slot)
            wait_rows(slot)
            acc = x_ref[...]
            for k in range(TOP_K):
                acc = acc + w_ref[k] * buf_ref[slot, k]
            o_ref[...] = _rms(acc, g_ref[...]).astype(o_ref.dtype)


def combine(x, y, pos, top_w, g, tt=128):
    n, d = x.shape
    tt = _tile(n, tt, 8)
    nt = n // tt
    pos3 = pos.reshape(nt, 1, tt * TOP_K)
    return pl.pallas_call(
        functools.partial(_combine_kernel, tt=tt, n_steps=nt),
        out_shape=jax.ShapeDtypeStruct((n, d), F32),
        grid=(nt,),
        in_specs=[pl.BlockSpec((1, 1, tt * TOP_K), lambda i: (i, 0, 0), memory_space=pltpu.SMEM),
                  pl.BlockSpec((1, 1, tt * TOP_K), lambda i: (jnp.minimum(i + 1, nt - 1), 0, 0),
                               memory_space=pltpu.SMEM),
                  pl.BlockSpec((tt, d), lambda i: (i, 0)),
                  pl.BlockSpec((TOP_K, tt, 1), lambda i: (0, i, 0)),
                  pl.BlockSpec(memory_space=pl.ANY),
                  pl.BlockSpec((1, d), lambda i: (0, 0))],
        out_specs=pl.BlockSpec((tt, d), lambda i: (i, 0)),
        scratch_shapes=[pltpu.VMEM((2, TOP_K, tt, d), F32), pltpu.SemaphoreType.DMA((2,))],
        compiler_params=_cparams(("arbitrary",)),
        name="moe_combine",
    )(pos3, pos3, x, top_w.T.reshape(TOP_K, n, 1), y, g.reshape(1, d).astype(F32))


def route_plan(top_idx, n_experts, tm):
    n = top_idx.shape[0]
    pairs = n * TOP_K
    p_rows = pairs + n_experts * tm
    e_flat = top_idx.reshape(pairs)
    onehot = (e_flat[:, None] == jnp.arange(n_experts)[None, :]).astype(jnp.int32)
    csum = jnp.cumsum(onehot, axis=0)
    rank = jnp.take_along_axis(csum, e_flat[:, None], axis=1)[:, 0] - 1
    counts = csum[-1]
    padded = ((counts + tm - 1) // tm) * tm
    ends = jnp.cumsum(padded)
    starts = ends - padded
    pos = starts[e_flat] + rank
    token = jnp.arange(pairs, dtype=jnp.int32) // TOP_K
    src = jnp.zeros((p_rows,), jnp.int32).at[pos].set(token)
    n_tiles = p_rows // tm
    tile_start = jnp.arange(n_tiles, dtype=jnp.int32) * tm
    tile_expert = jnp.minimum(jnp.sum((ends[None, :] <= tile_start[:, None]).astype(jnp.int32), axis=1), n_experts - 1)
    n_valid = (ends[-1] // tm).astype(jnp.int32).reshape(1)
    tile_expert = jnp.where(tile_start < ends[-1], tile_expert, tile_expert[jnp.maximum(n_valid[0] - 1, 0)])
    return pos.astype(jnp.int32), src, tile_expert.astype(jnp.int32), n_valid


def moe_block(x1, norm_ffn, w_router, b_router, w_gate_up, b_gate_up, w_down, b_down, norm_final, tm=512):
    n_experts = w_router.shape[1]
    top_idx, top_w = router(x1, norm_ffn, w_router, b_router)
    pos, src, tile_expert, n_valid = route_plan(top_idx, n_experts, tm)
    xs = gather_rows_rmsnorm(x1, norm_ffn, src)
    hid = expert_gate_up(xs, tile_expert, n_valid, w_gate_up, b_gate_up, tm)
    y = expert_down(hid, tile_expert, n_valid, w_down, b_down, tm)
    return combine(x1, y, pos, top_w, norm_final)


def kernel(x, norm_mix, w_in, w_up_sb, w_up_moba, w_branch_gate, b_branch_gate, w_out, norm_ffn,
           w_router, b_router, w_gate_up, b_gate_up, w_down, b_down, norm_final):
    b, t, d = x.shape
    n = b * t
    assert norm_mix.shape[0] == 1, "single-layer block"
    x2 = x.reshape(n, d)
    c_sb = w_up_sb.shape[1] // HEAD_DIM
    c_mb = w_up_moba.shape[1] // HEAD_DIM
    h = rmsnorm(x2, norm_mix[0], BF16)
    proj = matmul(h, w_in[0].astype(BF16), BF16)
    gates = matmul(h, w_branch_gate[0].astype(BF16), BF16, "sigmoid_bias", b_branch_gate[0])
    o_sb = sb_attention(proj, b, t, c_sb, 0, c_sb, 2 * c_sb)
    o_mb = moba_attention(proj, b, t, c_mb, 3 * c_sb, 3 * c_sb + c_mb, 3 * c_sb + 2 * c_mb)
    mixed = branch_mix(o_sb, o_mb, w_up_sb[0].astype(BF16), w_up_moba[0].astype(BF16), gates)
    x1 = matmul(mixed, w_out[0].astype(BF16), F32, "residual", x2)
    out = moe_block(x1, norm_ffn[0], w_router[0], b_router[0], w_gate_up[0], b_gate_up[0],
                    w_down[0], b_down[0], norm_final)
    return out.reshape(b, t, d)
```

```python
import functools
import math

import jax
import jax.numpy as jnp
import numpy as np
from jax import lax
from jax.experimental import pallas as pl
from jax.experimental.pallas import tpu as pltpu

F32 = jnp.float32
BF16 = jnp.bfloat16

HEAD_DIM = 128
MOBA_BLOCK = 256
MOBA_TOPK = 3
TOP_K = 4
SWIGLU_ALPHA = 1.702
SWIGLU_LIMIT = 7.0
RMS_EPS = 1e-5
LOG2E = 1.4426950408889634
LANES = 128
NEG_BIG = -1e30
VMEM_LIMIT = 56 * 1024 * 1024
N_PEN_COLS = 32
AUG_BIAS_COLS = 6
SB_DONE_BITS = 160.0
POS_SPLIT = 64


def _bf16_pieces(x, n):
    out, r = [], np.asarray(x, np.float32)
    for _ in range(n):
        piece = (r.view(np.uint32) & np.uint32(0xFFFF0000)).view(np.float32)
        out.append(piece)
        r = (r - piece).astype(np.float32)
    return out


def _cparams(sem):
    return pltpu.CompilerParams(dimension_semantics=sem, vmem_limit_bytes=VMEM_LIMIT)


def _tile(dim, target, align=LANES):
    if dim <= target:
        return dim
    t = (target // align) * align
    while t > align and dim % t:
        t -= align
    assert dim % t == 0, (dim, target)
    return t


def _rms(x, g):
    ms = jnp.mean(x * x, axis=-1, keepdims=True)
    return x * lax.rsqrt(ms + RMS_EPS) * g


def _rmsnorm_kernel(x_ref, g_ref, o_ref):
    o_ref[...] = _rms(x_ref[...].astype(F32), g_ref[...]).astype(o_ref.dtype)


def rmsnorm(x, g, out_dtype, tm=256):
    n, d = x.shape
    tm = _tile(n, tm, 8)
    return pl.pallas_call(
        _rmsnorm_kernel,
        out_shape=jax.ShapeDtypeStruct((n, d), out_dtype),
        grid=(n // tm,),
        in_specs=[pl.BlockSpec((tm, d), lambda i: (i, 0)),
                  pl.BlockSpec((1, d), lambda i: (0, 0))],
        out_specs=pl.BlockSpec((tm, d), lambda i: (i, 0)),
        compiler_params=_cparams(("parallel",)),
        name="rmsnorm",
    )(x, g.reshape(1, d).astype(F32))


def _mm_kernel(*refs, epilogue, nk):
    if epilogue == "none":
        a_ref, b_ref, o_ref, acc_ref = refs
        e_ref = None
    else:
        a_ref, b_ref, e_ref, o_ref, acc_ref = refs
    k = pl.program_id(2)

    @pl.when(k == 0)
    def _():
        acc_ref[...] = jnp.zeros_like(acc_ref)

    acc_ref[...] += jnp.dot(a_ref[...], b_ref[...], preferred_element_type=F32)

    @pl.when(k == nk - 1)
    def _():
        acc = acc_ref[...]
        if epilogue == "sigmoid_bias":
            acc = jax.nn.sigmoid(acc + e_ref[...])
        elif epilogue == "residual":
            acc = acc + e_ref[...]
        o_ref[...] = acc.astype(o_ref.dtype)


def matmul(a, b, out_dtype, epilogue="none", extra=None, tm=1024, tn=1024, tk=1024):
    m, kdim = a.shape
    _, n = b.shape
    tm, tn, tk = _tile(m, tm), _tile(n, tn), _tile(kdim, tk)
    nk = kdim // tk
    in_specs = [pl.BlockSpec((tm, tk), lambda i, j, k: (i, k)),
                pl.BlockSpec((tk, tn), lambda i, j, k: (k, j))]
    args = [a, b]
    if epilogue == "sigmoid_bias":
        in_specs.append(pl.BlockSpec((1, tn), lambda i, j, k: (0, j)))
        args.append(extra.reshape(1, n).astype(F32))
    elif epilogue == "residual":
        in_specs.append(pl.BlockSpec((tm, tn), lambda i, j, k: (i, j)))
        args.append(extra)
    return pl.pallas_call(
        functools.partial(_mm_kernel, epilogue=epilogue, nk=nk),
        out_shape=jax.ShapeDtypeStruct((m, n), out_dtype),
        grid=(m // tm, n // tn, nk),
        in_specs=in_specs,
        out_specs=pl.BlockSpec((tm, tn), lambda i, j, k: (i, j)),
        scratch_shapes=[pltpu.VMEM((tm, tn), F32)],
        compiler_params=_cparams(("parallel", "parallel", "arbitrary")),
        name="matmul_" + epilogue,
    )(*args)


def _mix_kernel(osb_ref, omb_ref, wsb_ref, wmb_ref, g0_ref, g1_ref, o_ref):
    u_sb = jnp.dot(osb_ref[...], wsb_ref[...], preferred_element_type=F32)
    u_mb = jnp.dot(omb_ref[...], wmb_ref[...], preferred_element_type=F32)
    mixed = g0_ref[...].astype(F32) * u_sb + g1_ref[...].astype(F32) * u_mb
    o_ref[...] = mixed.astype(o_ref.dtype)


def branch_mix(o_sb, o_mb, w_sb, w_mb, gates, tm=512, tn=1024):
    m, kdim = o_sb.shape
    d = w_sb.shape[1]
    tm, tn = _tile(m, tm), _tile(d, tn)
    nj = d // tn
    return pl.pallas_call(
        _mix_kernel,
        out_shape=jax.ShapeDtypeStruct((m, d), BF16),
        grid=(m // tm, nj),
        in_specs=[pl.BlockSpec((tm, kdim), lambda i, j: (i, 0)),
                  pl.BlockSpec((tm, kdim), lambda i, j: (i, 0)),
                  pl.BlockSpec((kdim, tn), lambda i, j: (0, j)),
                  pl.BlockSpec((kdim, tn), lambda i, j: (0, j)),
                  pl.BlockSpec((tm, tn), lambda i, j: (i, j)),
                  pl.BlockSpec((tm, tn), lambda i, j: (i, nj + j))],
        out_specs=pl.BlockSpec((tm, tn), lambda i, j: (i, j)),
        compiler_params=_cparams(("parallel", "parallel")),
        name="branch_mix",
    )(o_sb, o_mb, w_sb, w_mb, gates, gates)


def _softplus2(z):
    bits = lax.bitcast_convert_type(z, jnp.uint32) | jnp.uint32(0x80000000)
    neg_abs = lax.bitcast_convert_type(bits, F32)
    return jnp.maximum(z, 0.0) + jnp.log2(1.0 + jnp.exp2(neg_abs))


def _sb_kernel(q_ref, k_ref, v_ref, u_ref, o_ref, acc_ref, run_ref, *, tq, tk, nblk, scale):
    i = pl.program_id(2)
    n_sub = tq // tk
    q = (q_ref[...].astype(F32) * (scale * LOG2E)).astype(BF16)
    dn = (((1,), (1,)), ((), ()))

    def group(j_hi, n, masked):
        lo = j_hi - (n - 1)
        start = pl.multiple_of(lo * tk, tk)
        kb = k_ref[pl.ds(start, n * tk), :]
        vb = v_ref[pl.ds(start, n * tk), :]
        z = lax.dot_general(q, kb, dn, preferred_element_type=F32)
        sp = _softplus2(z)
        if masked:
            row = lax.broadcasted_iota(jnp.int32, (tq, n * tk), 0) + i * tq
            col = lax.broadcasted_iota(jnp.int32, (tq, n * tk), 1) + lo * tk
            past = col < row
            sp = jnp.where(past, sp, 0.0)
        spb = sp.astype(BF16)
        run = run_ref[...]
        parts = [None] * n
        for b in range(n - 1, -1, -1):
            c = jnp.dot(spb[:, b * tk:(b + 1) * tk], u_ref[...], preferred_element_type=F32)
            later = jnp.concatenate([run] * (tk // LANES), axis=1)
            parts[b] = c + later
            run = run + jnp.broadcast_to(c[:, 0:1], run.shape)
        w = jnp.exp2(z - jnp.concatenate(parts, axis=1))
        if masked:
            w = jnp.where(past, w, 0.0)
        acc_ref[...] += jnp.dot(w.astype(BF16), vb, preferred_element_type=F32)
        run_ref[...] = run

    run_ref[...] = jnp.zeros_like(run_ref)
    acc_ref[...] = jnp.zeros_like(acc_ref)

    def diag_body(g, carry):
        group((i + 1) * n_sub - 1 - g * nblk, nblk, True)
        return carry

    lax.fori_loop(0, n_sub // nblk, diag_body, 0)

    n_groups = (i * n_sub) // nblk

    def cond(state):
        g, min_run = state
        return jnp.logical_and(g < n_groups, min_run < SB_DONE_BITS)

    def body(state):
        g, _ = state
        group(i * n_sub - 1 - g * nblk, nblk, False)
        return g + 1, jnp.min(run_ref[...])

    lax.while_loop(cond, body, (jnp.int32(0), jnp.min(run_ref[...])))
    o_ref[...] = acc_ref[...].astype(o_ref.dtype)


def sb_attention(proj, batch, seq, n_heads, q_col, k_col, v_col, tq=1024, tk=256, nblk=2):
    d = HEAD_DIM
    tq = _tile(seq, tq)
    tk = _tile(tq, tk)
    nq = seq // tq
    assert (tq // tk) % nblk == 0
    scale = 1.0 / math.sqrt(d)
    r = jnp.arange(tk)
    u = (r[:, None] >= r[None, :]).astype(BF16)
    return pl.pallas_call(
        functools.partial(_sb_kernel, tq=tq, tk=tk, nblk=nblk, scale=scale),
        out_shape=jax.ShapeDtypeStruct((batch * seq, n_heads * d), BF16),
        grid=(batch, n_heads, nq),
        in_specs=[pl.BlockSpec((tq, d), lambda b, h, i: (b * nq + i, q_col + h)),
                  pl.BlockSpec((seq, d), lambda b, h, i: (b, k_col + h)),
                  pl.BlockSpec((seq, d), lambda b, h, i: (b, v_col + h)),
                  pl.BlockSpec((tk, tk), lambda b, h, i: (0, 0))],
        out_specs=pl.BlockSpec((tq, d), lambda b, h, i: (b * nq + i, h)),
        scratch_shapes=[pltpu.VMEM((tq, d), F32), pltpu.VMEM((tq, LANES), F32)],
        compiler_params=_cparams(("parallel", "parallel", "arbitrary")),
        name="sb_attention",
    )(proj, proj, proj, u)


def _moba_kernel(q_ref, k_ref, v_ref, aug_ref, cst_ref, pool_ref, o_ref,
                 kaug_ref, vaug_ref, kmean_ref, acc_ref, m_ref, s0_ref, s1_ref, *, blk, tq, tkc, scale):
    i = pl.program_id(2)
    d = HEAD_DIM

    @pl.when(i == 0)
    def _():
        kaug_ref[:, :d] = k_ref[...]
        kaug_ref[:, d:] = aug_ref[...]
        vaug_ref[:, :d] = v_ref[...]
        vaug_ref[:, d:] = jnp.ones((vaug_ref.shape[0], d), BF16)
        kmean_ref[...] = jnp.dot(pool_ref[...], k_ref[...], preferred_element_type=F32)

    q_raw = q_ref[...]
    dn = (((1,), (1,)), ((), ()))

    km = kmean_ref[...]
    km_hi = km.astype(BF16)
    km_lo = (km - km_hi.astype(F32)).astype(BF16)
    gate = (lax.dot_general(q_raw, km_hi, dn, preferred_element_type=F32)
            + lax.dot_general(q_raw, km_lo, dn, preferred_element_type=F32))
    lane = lax.broadcasted_iota(jnp.int32, (tq, LANES), 1)
    row_blk = lax.shift_right_logical(lax.broadcasted_iota(jnp.int32, (tq, LANES), 0), int(math.log2(blk)))
    own = i * (tq // blk) + row_blk
    is_past = lane < own
    gate = jnp.where(is_past, gate, -jnp.inf)
    pen = jnp.where(is_past, NEG_BIG, 0.0)
    for r in range(MOBA_TOPK):
        best = jnp.max(gate, axis=-1, keepdims=True)
        first = jnp.min(jnp.where(gate == best, lane, LANES), axis=-1, keepdims=True)
        hit = (lane == first) & is_past
        pen = jnp.where(hit, 0.0, pen)
        gate = jnp.where(lane == first, -jnp.inf, gate)

    qs = (q_raw.astype(F32) * (scale * LOG2E)).astype(BF16)
    aux = jnp.where(lane < N_PEN_COLS, pen, cst_ref[0])
    q_aug = jnp.concatenate([qs, aux.astype(BF16)], axis=1)

    def logits(c, s_ref, diagonal):
        start = pl.multiple_of(c * tkc, tkc)
        kb = kaug_ref[pl.ds(start, tkc), :]
        s = lax.dot_general(q_aug, kb, dn, preferred_element_type=F32)
        if diagonal:
            qpos = lax.broadcasted_iota(jnp.int32, (tq, tkc), 0) + i * tq
            kpos = lax.broadcasted_iota(jnp.int32, (tq, tkc), 1) + c * tkc
            s = jnp.where(kpos <= qpos, s, NEG_BIG)
        s_ref[...] = s

    def softmax_pv(c, s_ref):
        start = pl.multiple_of(c * tkc, tkc)
        vb = vaug_ref[pl.ds(start, tkc), :]
        s = s_ref[...]
        m_old = m_ref[...]
        m_new = jnp.maximum(m_old, jnp.max(s, axis=-1, keepdims=True))
        alpha = jnp.exp2(m_old - m_new)
        p = jnp.exp2(s - m_new)
        acc_ref[...] = alpha * acc_ref[...] + jnp.dot(p.astype(BF16), vb, preferred_element_type=F32)
        m_ref[...] = m_new

    m_ref[...] = jnp.full(m_ref.shape, NEG_BIG, F32)
    acc_ref[...] = jnp.zeros_like(acc_ref)
    c_own = (i * tq) // tkc
    logits(c_own, s0_ref, True)
    pairs = c_own // 2

    def body(g, carry):
        c = c_own - 2 * g
        logits(c - 1, s1_ref, False)
        softmax_pv(c, s0_ref)
        logits(c - 2, s0_ref, False)
        softmax_pv(c - 1, s1_ref)
        return carry

    lax.fori_loop(0, pairs, body, 0)
    c_tail = c_own - 2 * pairs

    @pl.when(c_tail == 1)
    def _():
        logits(0, s1_ref, False)
        softmax_pv(1, s0_ref)
        softmax_pv(0, s1_ref)

    @pl.when(c_tail == 0)
    def _():
        softmax_pv(0, s0_ref)

    acc = acc_ref[...]
    o_ref[...] = (acc[:, :d] / acc[:, d:]).astype(o_ref.dtype)


def moba_attention(proj, batch, seq, n_heads, q_col, k_col, v_col, tq=1024, tkc=1024):
    d = HEAD_DIM
    blk = MOBA_BLOCK
    nb = seq // blk
    assert seq % blk == 0 and nb <= N_PEN_COLS
    tkc = _tile(seq, tkc, blk)
    tq = _tile(tkc, tq, blk)
    nq = seq // tq
    scale = 1.0 / math.sqrt(d)
    slopes = np.exp2(-8.0 * np.arange(1, n_heads + 1, dtype=np.float32) / n_heads).astype(np.float32)
    pieces = _bf16_pieces(slopes * np.float32(LOG2E), 3)
    cst = np.zeros((n_heads, 1, LANES), np.float32)
    for j in range(AUG_BIAS_COLS):
        cst[:, 0, N_PEN_COLS + j] = pieces[j % 3]
    pos = jnp.arange(seq, dtype=jnp.int32)
    onehot = ((pos[:, None] // blk) == jnp.arange(N_PEN_COLS)[None, :]).astype(BF16)
    pos_hi = ((pos // POS_SPLIT) * POS_SPLIT).astype(BF16)
    pos_lo = (pos % POS_SPLIT).astype(BF16)
    aug = jnp.concatenate([
        onehot,
        jnp.stack([pos_hi] * 3 + [pos_lo] * 3, axis=-1),
        jnp.zeros((seq, d - N_PEN_COLS - AUG_BIAS_COLS), BF16)], axis=-1)
    pool = (((jnp.arange(seq)[None, :] // blk) == jnp.arange(LANES)[:, None]).astype(F32) * (1.0 / blk)).astype(BF16)
    return pl.pallas_call(
        functools.partial(_moba_kernel, blk=blk, tq=tq, tkc=tkc, scale=scale),
        out_shape=jax.ShapeDtypeStruct((batch * seq, n_heads * d), BF16),
        grid=(batch, n_heads, nq),
        in_specs=[pl.BlockSpec((tq, d), lambda b, h, i: (b * nq + i, q_col + h)),
                  pl.BlockSpec((seq, d), lambda b, h, i: (b, k_col + h)),
                  pl.BlockSpec((seq, d), lambda b, h, i: (b, v_col + h)),
                  pl.BlockSpec((seq, d), lambda b, h, i: (0, 0)),
                  pl.BlockSpec((1, 1, LANES), lambda b, h, i: (h, 0, 0)),
                  pl.BlockSpec((LANES, seq), lambda b, h, i: (0, 0))],
        out_specs=pl.BlockSpec((tq, d), lambda b, h, i: (b * nq + i, h)),
        scratch_shapes=[pltpu.VMEM((seq, 2 * d), BF16), pltpu.VMEM((seq, 2 * d), BF16),
                        pltpu.VMEM((LANES, d), F32), pltpu.VMEM((tq, 2 * d), F32),
                        pltpu.VMEM((tq, 1), F32), pltpu.VMEM((tq, tkc), F32), pltpu.VMEM((tq, tkc), F32)],
        compiler_params=_cparams(("parallel", "parallel", "arbitrary")),
        name="moba_attention",
    )(proj, proj, proj, aug, jnp.asarray(cst), pool)


def _split(x):
    hi = x.astype(BF16)
    lo = (x - hi.astype(F32)).astype(BF16)
    return hi, lo


def _router_kernel(x_ref, g_ref, w_ref, b_ref, idx_ref, wgt_ref, h3_ref, *, n_experts):
    h = _rms(x_ref[...], g_ref[...])
    h3_ref[...] = h.reshape(h3_ref.shape).astype(h3_ref.dtype)
    h_hi, h_lo = _split(h)
    w_hi, w_lo = _split(w_ref[...])
    logits = (jnp.dot(h_hi, w_hi, preferred_element_type=F32)
              + jnp.dot(h_lo, w_hi, preferred_element_type=F32)
              + jnp.dot(h_hi, w_lo, preferred_element_type=F32)) + b_ref[...]
    tm = logits.shape[0]
    lane = lax.broadcasted_iota(jnp.int32, (tm, LANES), 1)
    logits = jnp.where(lane < n_experts, logits, -jnp.inf)
    idx_out = jnp.zeros((tm, LANES), jnp.int32)
    val_out = jnp.full((tm, LANES), -jnp.inf, F32)
    top = None
    for r in range(TOP_K):
        best = jnp.max(logits, axis=-1, keepdims=True)
        first = jnp.min(jnp.where(logits == best, lane, LANES), axis=-1, keepdims=True)
        if r == 0:
            top = best
        idx_out = jnp.where(lane == r, first, idx_out)
        val_out = jnp.where(lane == r, best, val_out)
        logits = jnp.where(lane == first, -jnp.inf, logits)
    e = jnp.exp(val_out - top)
    idx_ref[...] = idx_out
    wgt_ref[...] = e / jnp.sum(e, axis=-1, keepdims=True)


def router(x, g, w_router, b_router, tm=256):
    n, d = x.shape
    n_experts = w_router.shape[1]
    tm = _tile(n, tm, 8)
    w_pad = jnp.zeros((d, LANES), F32).at[:, :n_experts].set(w_router)
    b_pad = jnp.zeros((1, LANES), F32).at[0, :n_experts].set(b_router)
    idx, wgt, h3 = pl.pallas_call(
        functools.partial(_router_kernel, n_experts=n_experts),
        out_shape=(jax.ShapeDtypeStruct((n, LANES), jnp.int32),
                   jax.ShapeDtypeStruct((n, LANES), F32),
                   jax.ShapeDtypeStruct((n, d // LANES, LANES), BF16)),
        grid=(n // tm,),
        in_specs=[pl.BlockSpec((tm, d), lambda i: (i, 0)),
                  pl.BlockSpec((1, d), lambda i: (0, 0)),
                  pl.BlockSpec((d, LANES), lambda i: (0, 0)),
                  pl.BlockSpec((1, LANES), lambda i: (0, 0))],
        out_specs=(pl.BlockSpec((tm, LANES), lambda i: (i, 0)),
                   pl.BlockSpec((tm, LANES), lambda i: (i, 0)),
                   pl.BlockSpec((tm, d // LANES, LANES), lambda i: (i, 0, 0))),
        compiler_params=_cparams(("parallel",)),
        name="router",
    )(x, g.reshape(1, d).astype(F32), w_pad, b_pad)
    return idx[:, :TOP_K], wgt[:, :TOP_K], h3


def _row_copy(src_hbm, dst_vmem, sem, src_row, dst_row):
    return pltpu.make_async_copy(src_hbm.at[pl.ds(src_row, 1)], dst_vmem.at[pl.ds(dst_row, 1)], sem)


def _gather_kernel(cur_ref, nxt_ref, h_hbm, o_ref, buf_ref, sem, *, rows, n_steps):
    s = pl.program_id(0)

    def start_rows(idx_ref, slot):
        def start(r, c):
            _row_copy(h_hbm, buf_ref.at[slot], sem.at[slot], idx_ref[0, 0, r], r).start()
            return c
        lax.fori_loop(0, rows, start, 0, unroll=8)

    def wait_rows(slot):
        def wait(r, c):
            _row_copy(h_hbm, buf_ref.at[slot], sem.at[slot], 0, r).wait()
            return c
        lax.fori_loop(0, rows, wait, 0, unroll=8)

    @pl.when(s == 0)
    def _():
        start_rows(cur_ref, 0)

    for slot in range(2):
        @pl.when(lax.rem(s, 2) == slot)
        def _():
            @pl.when(s + 1 < n_steps)
            def _():
                start_rows(nxt_ref, 1 - slot)
            wait_rows(slot)
            o_ref[...] = buf_ref[slot].astype(F32).reshape(o_ref.shape).astype(o_ref.dtype)


def gather_rows(h3, src, rows=256):
    n, dc, lanes = h3.shape
    d = dc * lanes
    p = src.shape[0]
    rows = _tile(p, rows, 8)
    nt = p // rows
    src3 = src.reshape(nt, 1, rows)
    return pl.pallas_call(
        functools.partial(_gather_kernel, rows=rows, n_steps=nt),
        out_shape=jax.ShapeDtypeStruct((p, d), BF16),
        grid=(nt,),
        in_specs=[pl.BlockSpec((1, 1, rows), lambda i: (i, 0, 0), memory_space=pltpu.SMEM),
                  pl.BlockSpec((1, 1, rows), lambda i: (jnp.minimum(i + 1, nt - 1), 0, 0), memory_space=pltpu.SMEM),
                  pl.BlockSpec(memory_space=pl.ANY)],
        out_specs=pl.BlockSpec((rows, d), lambda i: (i, 0)),
        scratch_shapes=[pltpu.VMEM((2, rows, dc, lanes), BF16), pltpu.SemaphoreType.DMA((2,))],
        compiler_params=_cparams(("arbitrary",)),
        name="moe_gather",
    )(src3, src3, h3)


def _new_expert(te_ref, t):
    return jnp.logical_or(t == 0, te_ref[t] != te_ref[jnp.maximum(t - 1, 0)])


def _gate_up_kernel(te_ref, nv_ref, x_ref, wg_ref, wu_ref, bg_ref, bu_ref, o_ref, wgb_ref, wub_ref):
    t = pl.program_id(1)

    @pl.when(t < nv_ref[0])
    def _():
        @pl.when(_new_expert(te_ref, t))
        def _():
            wgb_ref[...] = wg_ref[0].astype(BF16)
            wub_ref[...] = wu_ref[0].astype(BF16)

        x = x_ref[...]
        g = jnp.dot(x, wgb_ref[...], preferred_element_type=F32) + bg_ref[0]
        u = jnp.dot(x, wub_ref[...], preferred_element_type=F32) + bu_ref[0]
        g = jnp.minimum(g, SWIGLU_LIMIT)
        u = jnp.clip(u, -SWIGLU_LIMIT, SWIGLU_LIMIT)
        o_ref[...] = (g * jax.nn.sigmoid(SWIGLU_ALPHA * g) * (u + 1.0)).astype(o_ref.dtype)

    @pl.when(t >= nv_ref[0])
    def _():
        o_ref[...] = jnp.zeros_like(o_ref)


def expert_gate_up(xs, tile_expert, n_valid, w_gate_up, b_gate_up, tm, tf=512):
    p, d = xs.shape
    n_e, _, f2 = w_gate_up.shape
    f = f2 // 2
    tf = _tile(f, tf)
    nf = f // tf
    nt = p // tm
    b3 = b_gate_up.reshape(n_e, 1, f2)

    def row_tile(c, t, te, nv):
        return jnp.minimum(t, nv[0] - 1)

    grid_spec = pltpu.PrefetchScalarGridSpec(
        num_scalar_prefetch=2,
        grid=(nf, nt),
        in_specs=[pl.BlockSpec((tm, d), lambda c, t, te, nv: (row_tile(c, t, te, nv), 0)),
                  pl.BlockSpec((1, d, tf), lambda c, t, te, nv: (te[t], 0, c)),
                  pl.BlockSpec((1, d, tf), lambda c, t, te, nv: (te[t], 0, nf + c)),
                  pl.BlockSpec((1, 1, tf), lambda c, t, te, nv: (te[t], 0, c)),
                  pl.BlockSpec((1, 1, tf), lambda c, t, te, nv: (te[t], 0, nf + c))],
        out_specs=pl.BlockSpec((tm, tf), lambda c, t, te, nv: (t, c)),
        scratch_shapes=[pltpu.VMEM((d, tf), BF16), pltpu.VMEM((d, tf), BF16)],
    )
    return pl.pallas_call(
        _gate_up_kernel,
        out_shape=jax.ShapeDtypeStruct((p, f), BF16),
        grid_spec=grid_spec,
        compiler_params=_cparams(("arbitrary", "arbitrary")),
        name="moe_gate_up",
    )(tile_expert, n_valid, xs, w_gate_up, w_gate_up, b3, b3)


def _down_kernel(te_ref, nv_ref, h_ref, w_ref, b_ref, o_ref, wb_ref):
    t = pl.program_id(1)

    @pl.when(t < nv_ref[0])
    def _():
        @pl.when(_new_expert(te_ref, t))
        def _():
            wb_ref[...] = w_ref[0].astype(BF16)

        y = jnp.dot(h_ref[...], wb_ref[...], preferred_element_type=F32) + b_ref[0]
        o_ref[...] = y.astype(o_ref.dtype)

    @pl.when(t >= nv_ref[0])
    def _():
        o_ref[...] = jnp.zeros_like(o_ref)


def expert_down(hid, tile_expert, n_valid, w_down, b_down, tm, tn=2048):
    p, f = hid.shape
    n_e, _, d = w_down.shape
    tn = _tile(d, tn)
    nd = d // tn
    nt = p // tm
    b3 = b_down.reshape(n_e, 1, d)

    def row_tile(c, t, te, nv):
        return jnp.minimum(t, nv[0] - 1)

    grid_spec = pltpu.PrefetchScalarGridSpec(
        num_scalar_prefetch=2,
        grid=(nd, nt),
        in_specs=[pl.BlockSpec((tm, f), lambda c, t, te, nv: (row_tile(c, t, te, nv), 0)),
                  pl.BlockSpec((1, f, tn), lambda c, t, te, nv: (te[t], 0, c)),
                  pl.BlockSpec((1, 1, tn), lambda c, t, te, nv: (te[t], 0, c))],
        out_specs=pl.BlockSpec((tm, tn), lambda c, t, te, nv: (t, c)),
        scratch_shapes=[pltpu.VMEM((f, tn), BF16)],
    )
    return pl.pallas_call(
        _down_kernel,
        out_shape=jax.ShapeDtypeStruct((p, d), F32),
        grid_spec=grid_spec,
        compiler_params=_cparams(("arbitrary", "arbitrary")),
        name="moe_down",
    )(tile_expert, n_valid, hid, w_down, b3)


def _row_copy2(src_hbm, dst_vmem, sem, src_row, dst_row):
    return pltpu.make_async_copy(src_hbm.at[pl.ds(src_row, 1), :],
                                 dst_vmem.at[pl.ds(dst_row, 1), :], sem)


def _combine_kernel(cur_ref, nxt_ref, x_ref, w_ref, y_hbm, g_ref, o_ref, buf_ref, sem, *, tt, n_steps):
    s = pl.program_id(0)

    def start_rows(idx_ref, slot):
        def start(r, c):
            for k in range(TOP_K):
                _row_copy2(y_hbm, buf_ref.at[slot, k], sem.at[slot], idx_ref[0, 0, r * TOP_K + k], r).start()
            return c
        lax.fori_loop(0, tt, start, 0, unroll=2)

    def wait_rows(slot):
        def wait(r, c):
            for k in range(TOP_K):
                _row_copy2(y_hbm, buf_ref.at[slot, k], sem.at[slot], 0, r).wait()
            return c
        lax.fori_loop(0, tt, wait, 0, unroll=2)

    @pl.when(s == 0)
    def _():
        start_rows(cur_ref, 0)

    for slot in range(2):
        @pl.when(lax.rem(s, 2) == slot)
        def _():
            @pl.when(s + 1 < n_steps)
            def _():
                start_rows(nxt_ref, 1 - slot)
            wait_rows(slot)
            acc = x_ref[...]
            for k in range(TOP_K):
                acc = acc + w_ref[k] * buf_ref[slot, k]
            o_ref[...] = _rms(acc, g_ref[...]).astype(o_ref.dtype)


def combine(x, y, pos, top_w, g, tt=128):
    n, d = x.shape
    tt = _tile(n, tt, 8)
    nt = n // tt
    pos3 = pos.reshape(nt, 1, tt * TOP_K)
    return pl.pallas_call(
        functools.partial(_combine_kernel, tt=tt, n_steps=nt),
        out_shape=jax.ShapeDtypeStruct((n, d), F32),
        grid=(nt,),
        in_specs=[pl.BlockSpec((1, 1, tt * TOP_K), lambda i: (i, 0, 0), memory_space=pltpu.SMEM),
                  pl.BlockSpec((1, 1, tt * TOP_K), lambda i: (jnp.minimum(i + 1, nt - 1), 0, 0),
                               memory_space=pltpu.SMEM),
                  pl.BlockSpec((tt, d), lambda i: (i, 0)),
                  pl.BlockSpec((TOP_K, tt, 1), lambda i: (0, i, 0)),
                  pl.BlockSpec(memory_space=pl.ANY),
                  pl.BlockSpec((1, d), lambda i: (0, 0))],
        out_specs=pl.BlockSpec((tt, d), lambda i: (i, 0)),
        scratch_shapes=[pltpu.VMEM((2, TOP_K, tt, d), F32), pltpu.SemaphoreType.DMA((2,))],
        compiler_params=_cparams(("arbitrary",)),
        name="moe_combine",
    )(pos3, pos3, x, top_w.T.reshape(TOP_K, n, 1), y, g.reshape(1, d).astype(F32))


def route_plan(top_idx, n_experts, tm):
    n = top_idx.shape[0]
    pairs = n * TOP_K
    p_rows = pairs + n_experts * tm
    e_flat = top_idx.reshape(pairs)
    onehot = (e_flat[:, None] == jnp.arange(n_experts)[None, :]).astype(jnp.int32)
    csum = jnp.cumsum(onehot, axis=0)
    rank = jnp.take_along_axis(csum, e_flat[:, None], axis=1)[:, 0] - 1
    counts = csum[-1]
    padded = ((counts + tm - 1) // tm) * tm
    ends = jnp.cumsum(padded)
    starts = ends - padded
    pos = starts[e_flat] + rank
    token = jnp.arange(pairs, dtype=jnp.int32) // TOP_K
    src = jnp.zeros((p_rows,), jnp.int32).at[pos].set(token)
    n_tiles = p_rows // tm
    tile_start = jnp.arange(n_tiles, dtype=jnp.int32) * tm
    tile_expert = jnp.minimum(jnp.sum((ends[None, :] <= tile_start[:, None]).astype(jnp.int32), axis=1), n_experts - 1)
    n_valid = (ends[-1] // tm).astype(jnp.int32).reshape(1)
    tile_expert = jnp.where(tile_start < ends[-1], tile_expert, tile_expert[jnp.maximum(n_valid[0] - 1, 0)])
    return pos.astype(jnp.int32), src, tile_expert.astype(jnp.int32), n_valid


def moe_block(x1, norm_ffn, w_router, b_router, w_gate_up, b_gate_up, w_down, b_down, norm_final, tm=512):
    n_experts = w_router.shape[1]
    top_idx, top_w, h3 = router(x1, norm_ffn, w_router, b_router)
    pos, src, tile_expert, n_valid = route_plan(top_idx, n_experts, tm)
    xs = gather_rows(h3, src)
    hid = expert_gate_up(xs, tile_expert, n_valid, w_gate_up, b_gate_up, tm)
    y = expert_down(hid, tile_expert, n_valid, w_down, b_down, tm)
    return combine(x1, y, pos, top_w, norm_final)


def kernel(x, norm_mix, w_in, w_up_sb, w_up_moba, w_branch_gate, b_branch_gate, w_out, norm_ffn,
           w_router, b_router, w_gate_up, b_gate_up, w_down, b_down, norm_final):
    b, t, d = x.shape
    n = b * t
    assert norm_mix.shape[0] == 1, "single-layer block"
    x2 = x.reshape(n, d)
    c_sb = w_up_sb.shape[1] // HEAD_DIM
    c_mb = w_up_moba.shape[1] // HEAD_DIM
    h = rmsnorm(x2, norm_mix[0], BF16)
    proj = matmul(h, w_in[0].astype(BF16), BF16)
    gates = matmul(h, w_branch_gate[0].astype(BF16), BF16, "sigmoid_bias", b_branch_gate[0])
    o_sb = sb_attention(proj, b, t, c_sb, 0, c_sb, 2 * c_sb)
    o_mb = moba_attention(proj, b, t, c_mb, 3 * c_sb, 3 * c_sb + c_mb, 3 * c_sb + 2 * c_mb)
    mixed = branch_mix(o_sb, o_mb, w_up_sb[0].astype(BF16), w_up_moba[0].astype(BF16), gates)
    x1 = matmul(mixed, w_out[0].astype(BF16), F32, "residual", x2)
    out = moe_block(x1, norm_ffn[0], w_router[0], b_router[0], w_gate_up[0], b_gate_up[0],
                    w_down[0], b_down[0], norm_final)
    return out.reshape(b, t, d)
```

```python
import functools
import math

import jax
import jax.numpy as jnp
import numpy as np
from jax import lax
from jax.experimental import pallas as pl
from jax.experimental.pallas import tpu as pltpu

F32 = jnp.float32
BF16 = jnp.bfloat16

HEAD_DIM = 128
MOBA_BLOCK = 256
MOBA_TOPK = 3
TOP_K = 4
SWIGLU_ALPHA = 1.702
SWIGLU_LIMIT = 7.0
RMS_EPS = 1e-5
LOG2E = 1.4426950408889634
LANES = 128
NEG_BIG = -1e30
VMEM_LIMIT = 56 * 1024 * 1024
N_PEN_COLS = 32
AUG_BIAS_COLS = 6
SB_DONE_BITS = 160.0
POS_SPLIT = 64


def _bf16_pieces(x, n):
    out, r = [], np.asarray(x, np.float32)
    for _ in range(n):
        piece = (r.view(np.uint32) & np.uint32(0xFFFF0000)).view(np.float32)
        out.append(piece)
        r = (r - piece).astype(np.float32)
    return out


def _cparams(sem):
    return pltpu.CompilerParams(dimension_semantics=sem, vmem_limit_bytes=VMEM_LIMIT)


def _tile(dim, target, align=LANES):
    if dim <= target:
        return dim
    t = (target // align) * align
    while t > align and dim % t:
        t -= align
    assert dim % t == 0, (dim, target)
    return t


def _rms(x, g):
    ms = jnp.mean(x * x, axis=-1, keepdims=True)
    return x * lax.rsqrt(ms + RMS_EPS) * g


def _rmsnorm_kernel(x_ref, g_ref, o_ref):
    o_ref[...] = _rms(x_ref[...].astype(F32), g_ref[...]).astype(o_ref.dtype)


def rmsnorm(x, g, out_dtype, tm=256):
    n, d = x.shape
    tm = _tile(n, tm, 8)
    return pl.pallas_call(
        _rmsnorm_kernel,
        out_shape=jax.ShapeDtypeStruct((n, d), out_dtype),
        grid=(n // tm,),
        in_specs=[pl.BlockSpec((tm, d), lambda i: (i, 0)),
                  pl.BlockSpec((1, d), lambda i: (0, 0))],
        out_specs=pl.BlockSpec((tm, d), lambda i: (i, 0)),
        compiler_params=_cparams(("parallel",)),
        name="rmsnorm",
    )(x, g.reshape(1, d).astype(F32))


def _mm_kernel(*refs, epilogue):
    if epilogue == "none":
        a_ref, b_ref, o_ref = refs
        e_ref = None
    else:
        a_ref, b_ref, e_ref, o_ref = refs
    acc = jnp.dot(a_ref[...], b_ref[...], preferred_element_type=F32)
    if epilogue == "sigmoid_bias":
        acc = jax.nn.sigmoid(acc + e_ref[...])
    elif epilogue == "residual":
        acc = acc + e_ref[...]
    o_ref[...] = acc.astype(o_ref.dtype)


def matmul(a, b, out_dtype, epilogue="none", extra=None, tm=1024, tn=1024):
    m, kdim = a.shape
    _, n = b.shape
    tm, tn = _tile(m, tm), _tile(n, tn)
    in_specs = [pl.BlockSpec((tm, kdim), lambda i, j: (i, 0)),
                pl.BlockSpec((kdim, tn), lambda i, j: (0, j))]
    args = [a, b]
    if epilogue == "sigmoid_bias":
        in_specs.append(pl.BlockSpec((1, tn), lambda i, j: (0, j)))
        args.append(extra.reshape(1, n).astype(F32))
    elif epilogue == "residual":
        in_specs.append(pl.BlockSpec((tm, tn), lambda i, j: (i, j)))
        args.append(extra)
    return pl.pallas_call(
        functools.partial(_mm_kernel, epilogue=epilogue),
        out_shape=jax.ShapeDtypeStruct((m, n), out_dtype),
        grid=(m // tm, n // tn),
        in_specs=in_specs,
        out_specs=pl.BlockSpec((tm, tn), lambda i, j: (i, j)),
        compiler_params=_cparams(("parallel", "parallel")),
        name="matmul_" + epilogue,
    )(*args)


def _mix_kernel(osb_ref, omb_ref, wsb_ref, wmb_ref, g0_ref, g1_ref, o_ref):
    u_sb = jnp.dot(osb_ref[...], wsb_ref[...], preferred_element_type=F32)
    u_mb = jnp.dot(omb_ref[...], wmb_ref[...], preferred_element_type=F32)
    mixed = g0_ref[...].astype(F32) * u_sb + g1_ref[...].astype(F32) * u_mb
    o_ref[...] = mixed.astype(o_ref.dtype)


def branch_mix(o_sb, o_mb, w_sb, w_mb, gates, tm=512, tn=1024):
    m, kdim = o_sb.shape
    d = w_sb.shape[1]
    tm, tn = _tile(m, tm), _tile(d, tn)
    nj = d // tn
    return pl.pallas_call(
        _mix_kernel,
        out_shape=jax.ShapeDtypeStruct((m, d), BF16),
        grid=(m // tm, nj),
        in_specs=[pl.BlockSpec((tm, kdim), lambda i, j: (i, 0)),
                  pl.BlockSpec((tm, kdim), lambda i, j: (i, 0)),
                  pl.BlockSpec((kdim, tn), lambda i, j: (0, j)),
                  pl.BlockSpec((kdim, tn), lambda i, j: (0, j)),
                  pl.BlockSpec((tm, tn), lambda i, j: (i, j)),
                  pl.BlockSpec((tm, tn), lambda i, j: (i, nj + j))],
        out_specs=pl.BlockSpec((tm, tn), lambda i, j: (i, j)),
        compiler_params=_cparams(("parallel", "parallel")),
        name="branch_mix",
    )(o_sb, o_mb, w_sb, w_mb, gates, gates)


def _softplus2(z):
    bits = lax.bitcast_convert_type(z, jnp.uint32) | jnp.uint32(0x80000000)
    neg_abs = lax.bitcast_convert_type(bits, F32)
    return jnp.maximum(z, 0.0) + jnp.log2(1.0 + jnp.exp2(neg_abs))


def _sb_kernel(q_ref, k_ref, v_ref, u_ref, o_ref, acc_ref, run_ref, *, tq, tk, nblk, scale):
    i = pl.program_id(2)
    n_sub = tq // tk
    q = (q_ref[...].astype(F32) * (scale * LOG2E)).astype(BF16)
    dn = (((1,), (1,)), ((), ()))

    def group(j_hi, n, masked):
        lo = j_hi - (n - 1)
        start = pl.multiple_of(lo * tk, tk)
        kb = k_ref[pl.ds(start, n * tk), :]
        vb = v_ref[pl.ds(start, n * tk), :]
        z = lax.dot_general(q, kb, dn, preferred_element_type=F32)
        sp = _softplus2(z)
        if masked:
            row = lax.broadcasted_iota(jnp.int32, (tq, n * tk), 0) + i * tq
            col = lax.broadcasted_iota(jnp.int32, (tq, n * tk), 1) + lo * tk
            past = col < row
            sp = jnp.where(past, sp, 0.0)
        spb = sp.astype(BF16)
        run = run_ref[...]
        parts = [None] * n
        for b in range(n - 1, -1, -1):
            c = jnp.dot(spb[:, b * tk:(b + 1) * tk], u_ref[...], preferred_element_type=F32)
            later = jnp.concatenate([run] * (tk // LANES), axis=1)
            parts[b] = c + later
            run = run + jnp.broadcast_to(c[:, 0:1], run.shape)
        w = jnp.exp2(z - jnp.concatenate(parts, axis=1))
        if masked:
            w = jnp.where(past, w, 0.0)
        acc_ref[...] += jnp.dot(w.astype(BF16), vb, preferred_element_type=F32)
        run_ref[...] = run

    run_ref[...] = jnp.zeros_like(run_ref)
    acc_ref[...] = jnp.zeros_like(acc_ref)

    def diag_body(g, carry):
        group((i + 1) * n_sub - 1 - g * nblk, nblk, True)
        return carry

    lax.fori_loop(0, n_sub // nblk, diag_body, 0)

    n_groups = (i * n_sub) // nblk

    def cond(state):
        g, min_run = state
        return jnp.logical_and(g < n_groups, min_run < SB_DONE_BITS)

    def body(state):
        g, _ = state
        group(i * n_sub - 1 - g * nblk, nblk, False)
        return g + 1, jnp.min(run_ref[...])

    lax.while_loop(cond, body, (jnp.int32(0), jnp.min(run_ref[...])))
    o_ref[...] = acc_ref[...].astype(o_ref.dtype)


def sb_attention(proj, batch, seq, n_heads, q_col, k_col, v_col, tq=1024, tk=256, nblk=2):
    d = HEAD_DIM
    tq = _tile(seq, tq)
    tk = _tile(tq, tk)
    nq = seq // tq
    assert (tq // tk) % nblk == 0
    scale = 1.0 / math.sqrt(d)
    r = jnp.arange(tk)
    u = (r[:, None] >= r[None, :]).astype(BF16)
    return pl.pallas_call(
        functools.partial(_sb_kernel, tq=tq, tk=tk, nblk=nblk, scale=scale),
        out_shape=jax.ShapeDtypeStruct((batch * seq, n_heads * d), BF16),
        grid=(batch, n_heads, nq),
        in_specs=[pl.BlockSpec((tq, d), lambda b, h, i: (b * nq + i, q_col + h)),
                  pl.BlockSpec((seq, d), lambda b, h, i: (b, k_col + h)),
                  pl.BlockSpec((seq, d), lambda b, h, i: (b, v_col + h)),
                  pl.BlockSpec((tk, tk), lambda b, h, i: (0, 0))],
        out_specs=pl.BlockSpec((tq, d), lambda b, h, i: (b * nq + i, h)),
        scratch_shapes=[pltpu.VMEM((tq, d), F32), pltpu.VMEM((tq, LANES), F32)],
        compiler_params=_cparams(("parallel", "parallel", "arbitrary")),
        name="sb_attention",
    )(proj, proj, proj, u)


def _moba_kernel(q_ref, k_ref, v_ref, aug_ref, cst_ref, pool_ref, o_ref,
                 kaug_ref, vaug_ref, kmean_ref, acc_ref, m_ref, s0_ref, s1_ref, *, blk, tq, tkc, scale):
    i = pl.program_id(2)
    d = HEAD_DIM

    @pl.when(i == 0)
    def _():
        kaug_ref[:, :d] = k_ref[...]
        kaug_ref[:, d:] = aug_ref[...]
        vaug_ref[:, :d] = v_ref[...]
        vaug_ref[:, d:] = jnp.ones((vaug_ref.shape[0], d), BF16)
        kmean_ref[...] = jnp.dot(pool_ref[...], k_ref[...], preferred_element_type=F32)

    q_raw = q_ref[...]
    dn = (((1,), (1,)), ((), ()))

    km = kmean_ref[...]
    km_hi = km.astype(BF16)
    km_lo = (km - km_hi.astype(F32)).astype(BF16)
    gate = (lax.dot_general(q_raw, km_hi, dn, preferred_element_type=F32)
            + lax.dot_general(q_raw, km_lo, dn, preferred_element_type=F32))
    lane = lax.broadcasted_iota(jnp.int32, (tq, LANES), 1)
    row_blk = lax.shift_right_logical(lax.broadcasted_iota(jnp.int32, (tq, LANES), 0), int(math.log2(blk)))
    own = i * (tq // blk) + row_blk
    is_past = lane < own
    gate = jnp.where(is_past, gate, -jnp.inf)
    pen = jnp.where(is_past, NEG_BIG, 0.0)
    for r in range(MOBA_TOPK):
        best = jnp.max(gate, axis=-1, keepdims=True)
        first = jnp.min(jnp.where(gate == best, lane, LANES), axis=-1, keepdims=True)
        hit = (lane == first) & is_past
        pen = jnp.where(hit, 0.0, pen)
        gate = jnp.where(lane == first, -jnp.inf, gate)

    qs = (q_raw.astype(F32) * (scale * LOG2E)).astype(BF16)
    aux = jnp.where(lane < N_PEN_COLS, pen, cst_ref[0])
    q_aug = jnp.concatenate([qs, aux.astype(BF16)], axis=1)

    def logits(c, s_ref, diagonal):
        start = pl.multiple_of(c * tkc, tkc)
        kb = kaug_ref[pl.ds(start, tkc), :]
        s = lax.dot_general(q_aug, kb, dn, preferred_element_type=F32)
        if diagonal:
            qpos = lax.broadcasted_iota(jnp.int32, (tq, tkc), 0) + i * tq
            kpos = lax.broadcasted_iota(jnp.int32, (tq, tkc), 1) + c * tkc
            s = jnp.where(kpos <= qpos, s, NEG_BIG)
        s_ref[...] = s

    def softmax_pv(c, s_ref):
        start = pl.multiple_of(c * tkc, tkc)
        vb = vaug_ref[pl.ds(start, tkc), :]
        s = s_ref[...]
        m_old = m_ref[...]
        m_new = jnp.maximum(m_old, jnp.max(s, axis=-1, keepdims=True))
        alpha = jnp.exp2(m_old - m_new)
        p = jnp.exp2(s - m_new)
        acc_ref[...] = alpha * acc_ref[...] + jnp.dot(p.astype(BF16), vb, preferred_element_type=F32)
        m_ref[...] = m_new

    m_ref[...] = jnp.full(m_ref.shape, NEG_BIG, F32)
    acc_ref[...] = jnp.zeros_like(acc_ref)
    c_own = (i * tq) // tkc
    logits(c_own, s0_ref, True)
    pairs = c_own // 2

    def body(g, carry):
        c = c_own - 2 * g
        logits(c - 1, s1_ref, False)
        softmax_pv(c, s0_ref)
        logits(c - 2, s0_ref, False)
        softmax_pv(c - 1, s1_ref)
        return carry

    lax.fori_loop(0, pairs, body, 0)
    c_tail = c_own - 2 * pairs

    @pl.when(c_tail == 1)
    def _():
        logits(0, s1_ref, False)
        softmax_pv(1, s0_ref)
        softmax_pv(0, s1_ref)

    @pl.when(c_tail == 0)
    def _():
        softmax_pv(0, s0_ref)

    acc = acc_ref[...]
    o_ref[...] = (acc[:, :d] / acc[:, d:]).astype(o_ref.dtype)


def moba_attention(proj, batch, seq, n_heads, q_col, k_col, v_col, tq=1024, tkc=1024):
    d = HEAD_DIM
    blk = MOBA_BLOCK
    nb = seq // blk
    assert seq % blk == 0 and nb <= N_PEN_COLS
    tkc = _tile(seq, tkc, blk)
    tq = _tile(tkc, tq, blk)
    nq = seq // tq
    scale = 1.0 / math.sqrt(d)
    slopes = np.exp2(-8.0 * np.arange(1, n_heads + 1, dtype=np.float32) / n_heads).astype(np.float32)
    pieces = _bf16_pieces(slopes * np.float32(LOG2E), 3)
    cst = np.zeros((n_heads, 1, LANES), np.float32)
    for j in range(AUG_BIAS_COLS):
        cst[:, 0, N_PEN_COLS + j] = pieces[j % 3]
    pos = jnp.arange(seq, dtype=jnp.int32)
    onehot = ((pos[:, None] // blk) == jnp.arange(N_PEN_COLS)[None, :]).astype(BF16)
    pos_hi = ((pos // POS_SPLIT) * POS_SPLIT).astype(BF16)
    pos_lo = (pos % POS_SPLIT).astype(BF16)
    aug = jnp.concatenate([
        onehot,
        jnp.stack([pos_hi] * 3 + [pos_lo] * 3, axis=-1),
        jnp.zeros((seq, d - N_PEN_COLS - AUG_BIAS_COLS), BF16)], axis=-1)
    pool = (((jnp.arange(seq)[None, :] // blk) == jnp.arange(LANES)[:, None]).astype(F32) * (1.0 / blk)).astype(BF16)
    return pl.pallas_call(
        functools.partial(_moba_kernel, blk=blk, tq=tq, tkc=tkc, scale=scale),
        out_shape=jax.ShapeDtypeStruct((batch * seq, n_heads * d), BF16),
        grid=(batch, n_heads, nq),
        in_specs=[pl.BlockSpec((tq, d), lambda b, h, i: (b * nq + i, q_col + h)),
                  pl.BlockSpec((seq, d), lambda b, h, i: (b, k_col + h)),
                  pl.BlockSpec((seq, d), lambda b, h, i: (b, v_col + h)),
                  pl.BlockSpec((seq, d), lambda b, h, i: (0, 0)),
                  pl.BlockSpec((1, 1, LANES), lambda b, h, i: (h, 0, 0)),
                  pl.BlockSpec((LANES, seq), lambda b, h, i: (0, 0))],
        out_specs=pl.BlockSpec((tq, d), lambda b, h, i: (b * nq + i, h)),
        scratch_shapes=[pltpu.VMEM((seq, 2 * d), BF16), pltpu.VMEM((seq, 2 * d), BF16),
                        pltpu.VMEM((LANES, d), F32), pltpu.VMEM((tq, 2 * d), F32),
                        pltpu.VMEM((tq, 1), F32), pltpu.VMEM((tq, tkc), F32), pltpu.VMEM((tq, tkc), F32)],
        compiler_params=_cparams(("parallel", "parallel", "arbitrary")),
        name="moba_attention",
    )(proj, proj, proj, aug, jnp.asarray(cst), pool)


def _split(x):
    hi = x.astype(BF16)
    lo = (x - hi.astype(F32)).astype(BF16)
    return hi, lo


def _router_kernel(x_ref, g_ref, w_ref, b_ref, idx_ref, wgt_ref, h3_ref, *, n_experts):
    h = _rms(x_ref[...], g_ref[...])
    h3_ref[...] = h.reshape(h3_ref.shape).astype(h3_ref.dtype)
    h_hi, h_lo = _split(h)
    w_hi, w_lo = _split(w_ref[...])
    logits = (jnp.dot(h_hi, w_hi, preferred_element_type=F32)
              + jnp.dot(h_lo, w_hi, preferred_element_type=F32)
              + jnp.dot(h_hi, w_lo, preferred_element_type=F32)) + b_ref[...]
    tm = logits.shape[0]
    lane = lax.broadcasted_iota(jnp.int32, (tm, LANES), 1)
    logits = jnp.where(lane < n_experts, logits, -jnp.inf)
    idx_out = jnp.zeros((tm, LANES), jnp.int32)
    val_out = jnp.full((tm, LANES), -jnp.inf, F32)
    top = None
    for r in range(TOP_K):
        best = jnp.max(logits, axis=-1, keepdims=True)
        first = jnp.min(jnp.where(logits == best, lane, LANES), axis=-1, keepdims=True)
        if r == 0:
            top = best
        idx_out = jnp.where(lane == r, first, idx_out)
        val_out = jnp.where(lane == r, best, val_out)
        logits = jnp.where(lane == first, -jnp.inf, logits)
    e = jnp.exp(val_out - top)
    idx_ref[...] = idx_out
    wgt_ref[...] = e / jnp.sum(e, axis=-1, keepdims=True)


def router(x, g, w_router, b_router, tm=256):
    n, d = x.shape
    n_experts = w_router.shape[1]
    tm = _tile(n, tm, 8)
    w_pad = jnp.zeros((d, LANES), F32).at[:, :n_experts].set(w_router)
    b_pad = jnp.zeros((1, LANES), F32).at[0, :n_experts].set(b_router)
    idx, wgt, h3 = pl.pallas_call(
        functools.partial(_router_kernel, n_experts=n_experts),
        out_shape=(jax.ShapeDtypeStruct((n, LANES), jnp.int32),
                   jax.ShapeDtypeStruct((n, LANES), F32),
                   jax.ShapeDtypeStruct((n, d // LANES, LANES), BF16)),
        grid=(n // tm,),
        in_specs=[pl.BlockSpec((tm, d), lambda i: (i, 0)),
                  pl.BlockSpec((1, d), lambda i: (0, 0)),
                  pl.BlockSpec((d, LANES), lambda i: (0, 0)),
                  pl.BlockSpec((1, LANES), lambda i: (0, 0))],
        out_specs=(pl.BlockSpec((tm, LANES), lambda i: (i, 0)),
                   pl.BlockSpec((tm, LANES), lambda i: (i, 0)),
                   pl.BlockSpec((tm, d // LANES, LANES), lambda i: (i, 0, 0))),
        compiler_params=_cparams(("parallel",)),
        name="router",
    )(x, g.reshape(1, d).astype(F32), w_pad, b_pad)
    return idx[:, :TOP_K], wgt, h3


def _row_copy(src_hbm, dst_vmem, sem, src_row, dst_row):
    return pltpu.make_async_copy(src_hbm.at[pl.ds(src_row, 1)], dst_vmem.at[pl.ds(dst_row, 1)], sem)


def _gather_kernel(cur_ref, nxt_ref, h_hbm, o_ref, buf_ref, sem, *, rows, n_steps):
    s = pl.program_id(0)

    def start_rows(idx_ref, slot):
        def start(r8, c):
            for u in range(8):
                r = r8 * 8 + u
                _row_copy(h_hbm, buf_ref.at[slot], sem.at[slot], idx_ref[0, 0, r], r).start(priority=u % 2)
            return c
        lax.fori_loop(0, rows // 8, start, 0)

    def wait_rows(slot):
        def wait(r, c):
            _row_copy(h_hbm, buf_ref.at[slot], sem.at[slot], 0, r).wait()
            return c
        lax.fori_loop(0, rows, wait, 0, unroll=8)

    @pl.when(s == 0)
    def _():
        start_rows(cur_ref, 0)

    for slot in range(2):
        @pl.when(lax.rem(s, 2) == slot)
        def _():
            @pl.when(s + 1 < n_steps)
            def _():
                start_rows(nxt_ref, 1 - slot)
            wait_rows(slot)
            o_ref[...] = buf_ref[slot].astype(F32).reshape(o_ref.shape).astype(o_ref.dtype)


def gather_rows(h3, src, rows=256):
    n, dc, lanes = h3.shape
    d = dc * lanes
    p = src.shape[0]
    rows = _tile(p, rows, 8)
    nt = p // rows
    src3 = src.reshape(nt, 1, rows)
    return pl.pallas_call(
        functools.partial(_gather_kernel, rows=rows, n_steps=nt),
        out_shape=jax.ShapeDtypeStruct((p, d), BF16),
        grid=(nt,),
        in_specs=[pl.BlockSpec((1, 1, rows), lambda i: (i, 0, 0), memory_space=pltpu.SMEM),
                  pl.BlockSpec((1, 1, rows), lambda i: (jnp.minimum(i + 1, nt - 1), 0, 0), memory_space=pltpu.SMEM),
                  pl.BlockSpec(memory_space=pl.ANY)],
        out_specs=pl.BlockSpec((rows, d), lambda i: (i, 0)),
        scratch_shapes=[pltpu.VMEM((2, rows, dc, lanes), BF16), pltpu.SemaphoreType.DMA((2,))],
        compiler_params=_cparams(("arbitrary",)),
        name="moe_gather",
    )(src3, src3, h3)


def _new_expert(te_ref, t):
    return jnp.logical_or(t == 0, te_ref[t] != te_ref[jnp.maximum(t - 1, 0)])


def _gate_up_kernel(te_ref, nv_ref, x_ref, wg_ref, wu_ref, bg_ref, bu_ref, o_ref, wgb_ref, wub_ref):
    t = pl.program_id(1)

    @pl.when(t < nv_ref[0])
    def _():
        @pl.when(_new_expert(te_ref, t))
        def _():
            wgb_ref[...] = wg_ref[0].astype(BF16)
            wub_ref[...] = wu_ref[0].astype(BF16)

        x = x_ref[...]
        g = jnp.dot(x, wgb_ref[...], preferred_element_type=F32) + bg_ref[0]
        u = jnp.dot(x, wub_ref[...], preferred_element_type=F32) + bu_ref[0]
        g = jnp.minimum(g, SWIGLU_LIMIT)
        u = jnp.clip(u, -SWIGLU_LIMIT, SWIGLU_LIMIT)
        o_ref[...] = (g * jax.nn.sigmoid(SWIGLU_ALPHA * g) * (u + 1.0)).astype(o_ref.dtype)

    @pl.when(t >= nv_ref[0])
    def _():
        o_ref[...] = jnp.zeros_like(o_ref)


def expert_gate_up(xs, tile_expert, n_valid, w_gate_up, b_gate_up, tm, tf=512):
    p, d = xs.shape
    n_e, _, f2 = w_gate_up.shape
    f = f2 // 2
    tf = _tile(f, tf)
    nf = f // tf
    nt = p // tm
    b3 = b_gate_up.reshape(n_e, 1, f2)

    def row_tile(c, t, te, nv):
        return jnp.minimum(t, nv[0] - 1)

    grid_spec = pltpu.PrefetchScalarGridSpec(
        num_scalar_prefetch=2,
        grid=(nf, nt),
        in_specs=[pl.BlockSpec((tm, d), lambda c, t, te, nv: (row_tile(c, t, te, nv), 0)),
                  pl.BlockSpec((1, d, tf), lambda c, t, te, nv: (te[t], 0, c)),
                  pl.BlockSpec((1, d, tf), lambda c, t, te, nv: (te[t], 0, nf + c)),
                  pl.BlockSpec((1, 1, tf), lambda c, t, te, nv: (te[t], 0, c)),
                  pl.BlockSpec((1, 1, tf), lambda c, t, te, nv: (te[t], 0, nf + c))],
        out_specs=pl.BlockSpec((tm, tf), lambda c, t, te, nv: (t, c)),
        scratch_shapes=[pltpu.VMEM((d, tf), BF16), pltpu.VMEM((d, tf), BF16)],
    )
    return pl.pallas_call(
        _gate_up_kernel,
        out_shape=jax.ShapeDtypeStruct((p, f), BF16),
        grid_spec=grid_spec,
        compiler_params=_cparams(("arbitrary", "arbitrary")),
        name="moe_gate_up",
    )(tile_expert, n_valid, xs, w_gate_up, w_gate_up, b3, b3)


def _down_kernel(te_ref, nv_ref, h_ref, w_ref, b_ref, o_ref, wb_ref):
    t = pl.program_id(1)

    @pl.when(t < nv_ref[0])
    def _():
        @pl.when(_new_expert(te_ref, t))
        def _():
            wb_ref[...] = w_ref[0].astype(BF16)

        y = jnp.dot(h_ref[...], wb_ref[...], preferred_element_type=F32) + b_ref[0]
        o_ref[...] = y.astype(o_ref.dtype)

    @pl.when(t >= nv_ref[0])
    def _():
        o_ref[...] = jnp.zeros_like(o_ref)


def expert_down(hid, tile_expert, n_valid, w_down, b_down, tm, tn=2048):
    p, f = hid.shape
    n_e, _, d = w_down.shape
    tn = _tile(d, tn)
    nd = d // tn
    nt = p // tm
    b3 = b_down.reshape(n_e, 1, d)

    def row_tile(c, t, te, nv):
        return jnp.minimum(t, nv[0] - 1)

    grid_spec = pltpu.PrefetchScalarGridSpec(
        num_scalar_prefetch=2,
        grid=(nd, nt),
        in_specs=[pl.BlockSpec((tm, f), lambda c, t, te, nv: (row_tile(c, t, te, nv), 0)),
                  pl.BlockSpec((1, f, tn), lambda c, t, te, nv: (te[t], 0, c)),
                  pl.BlockSpec((1, 1, tn), lambda c, t, te, nv: (te[t], 0, c))],
        out_specs=pl.BlockSpec((tm, tn), lambda c, t, te, nv: (t, c)),
        scratch_shapes=[pltpu.VMEM((f, tn), BF16)],
    )
    return pl.pallas_call(
        _down_kernel,
        out_shape=jax.ShapeDtypeStruct((p, d), F32),
        grid_spec=grid_spec,
        compiler_params=_cparams(("arbitrary", "arbitrary")),
        name="moe_down",
    )(tile_expert, n_valid, hid, w_down, b3)


def _row_copy2(src_hbm, dst_vmem, sem, src_row, dst_row):
    return pltpu.make_async_copy(src_hbm.at[pl.ds(src_row, 1), :],
                                 dst_vmem.at[pl.ds(dst_row, 1), :], sem)


def _combine_kernel(cur_ref, nxt_ref, x_ref, w_ref, y_hbm, g_ref, o_ref, buf_ref, sem, *, tt, n_steps):
    s = pl.program_id(0)

    def start_rows(idx_ref, slot):
        def start(r, c):
            for k in range(TOP_K):
                _row_copy2(y_hbm, buf_ref.at[slot, k], sem.at[slot], idx_ref[0, 0, r * TOP_K + k], r).start()
            return c
        lax.fori_loop(0, tt, start, 0, unroll=2)

    def wait_rows(slot):
        def wait(r, c):
            for k in range(TOP_K):
                _row_copy2(y_hbm, buf_ref.at[slot, k], sem.at[slot], 0, r).wait()
            return c
        lax.fori_loop(0, tt, wait, 0, unroll=2)

    @pl.when(s == 0)
    def _():
        start_rows(cur_ref, 0)

    for slot in range(2):
        @pl.when(lax.rem(s, 2) == slot)
        def _():
            @pl.when(s + 1 < n_steps)
            def _():
                start_rows(nxt_ref, 1 - slot)
            wait_rows(slot)
            acc = x_ref[...]
            w = w_ref[...]
            for k in range(TOP_K):
                acc = acc + w[:, k:k + 1] * buf_ref[slot, k]
            o_ref[...] = _rms(acc, g_ref[...]).astype(o_ref.dtype)


def combine(x, y, pos, top_w, g, tt=128):
    n, d = x.shape
    tt = _tile(n, tt, 8)
    nt = n // tt
    pos3 = pos.reshape(nt, 1, tt * TOP_K)
    return pl.pallas_call(
        functools.partial(_combine_kernel, tt=tt, n_steps=nt),
        out_shape=jax.ShapeDtypeStruct((n, d), F32),
        grid=(nt,),
        in_specs=[pl.BlockSpec((1, 1, tt * TOP_K), lambda i: (i, 0, 0), memory_space=pltpu.SMEM),
                  pl.BlockSpec((1, 1, tt * TOP_K), lambda i: (jnp.minimum(i + 1, nt - 1), 0, 0),
                               memory_space=pltpu.SMEM),
                  pl.BlockSpec((tt, d), lambda i: (i, 0)),
                  pl.BlockSpec((tt, LANES), lambda i: (i, 0)),
                  pl.BlockSpec(memory_space=pl.ANY),
                  pl.BlockSpec((1, d), lambda i: (0, 0))],
        out_specs=pl.BlockSpec((tt, d), lambda i: (i, 0)),
        scratch_shapes=[pltpu.VMEM((2, TOP_K, tt, d), F32), pltpu.SemaphoreType.DMA((2,))],
        compiler_params=_cparams(("arbitrary",)),
        name="moe_combine",
    )(pos3, pos3, x, top_w, y, g.reshape(1, d).astype(F32))


def route_plan(top_idx, n_experts, tm):
    n = top_idx.shape[0]
    pairs = n * TOP_K
    p_rows = pairs + n_experts * tm
    e_flat = top_idx.reshape(pairs)
    onehot = (e_flat[:, None] == jnp.arange(n_experts)[None, :]).astype(jnp.int32)
    csum = jnp.cumsum(onehot, axis=0)
    rank = jnp.take_along_axis(csum, e_flat[:, None], axis=1)[:, 0] - 1
    counts = csum[-1]
    padded = ((counts + tm - 1) // tm) * tm
    ends = jnp.cumsum(padded)
    starts = ends - padded
    pos = starts[e_flat] + rank
    token = jnp.arange(pairs, dtype=jnp.int32) // TOP_K
    src = jnp.zeros((p_rows,), jnp.int32).at[pos].set(token)
    n_tiles = p_rows // tm
    tile_start = jnp.arange(n_tiles, dtype=jnp.int32) * tm
    tile_expert = jnp.minimum(jnp.sum((ends[None, :] <= tile_start[:, None]).astype(jnp.int32), axis=1), n_experts - 1)
    n_valid = (ends[-1] // tm).astype(jnp.int32).reshape(1)
    tile_expert = jnp.where(tile_start < ends[-1], tile_expert, tile_expert[jnp.maximum(n_valid[0] - 1, 0)])
    return pos.astype(jnp.int32), src, tile_expert.astype(jnp.int32), n_valid


def moe_block(x1, norm_ffn, w_router, b_router, w_gate_up, b_gate_up, w_down, b_down, norm_final, tm=512):
    n_experts = w_router.shape[1]
    top_idx, top_w, h3 = router(x1, norm_ffn, w_router, b_router)
    pos, src, tile_expert, n_valid = route_plan(top_idx, n_experts, tm)
    xs = gather_rows(h3, src)
    hid = expert_gate_up(xs, tile_expert, n_valid, w_gate_up, b_gate_up, tm)
    y = expert_down(hid, tile_expert, n_valid, w_down, b_down, tm)
    return combine(x1, y, pos, top_w, norm_final)


def kernel(x, norm_mix, w_in, w_up_sb, w_up_moba, w_branch_gate, b_branch_gate, w_out, norm_ffn,
           w_router, b_router, w_gate_up, b_gate_up, w_down, b_down, norm_final):
    b, t, d = x.shape
    n = b * t
    assert norm_mix.shape[0] == 1, "single-layer block"
    x2 = x.reshape(n, d)
    c_sb = w_up_sb.shape[1] // HEAD_DIM
    c_mb = w_up_moba.shape[1] // HEAD_DIM
    h = rmsnorm(x2, norm_mix[0], BF16)
    proj = matmul(h, w_in[0].astype(BF16), BF16)
    gates = matmul(h, w_branch_gate[0].astype(BF16), BF16, "sigmoid_bias", b_branch_gate[0])
    o_sb = sb_attention(proj, b, t, c_sb, 0, c_sb, 2 * c_sb)
    o_mb = moba_attention(proj, b, t, c_mb, 3 * c_sb, 3 * c_sb + c_mb, 3 * c_sb + 2 * c_mb)
    mixed = branch_mix(o_sb, o_mb, w_up_sb[0].astype(BF16), w_up_moba[0].astype(BF16), gates)
    x1 = matmul(mixed, w_out[0].astype(BF16), F32, "residual", x2, tm=512)
    out = moe_block(x1, norm_ffn[0], w_router[0], b_router[0], w_gate_up[0], b_gate_up[0],
                    w_down[0], b_down[0], norm_final)
    return out.reshape(b, t, d)
```

```python
import functools
import math

import jax
import jax.numpy as jnp
import numpy as np
from jax import lax
from jax.experimental import pallas as pl
from jax.experimental.pallas import tpu as pltpu

F32 = jnp.float32
BF16 = jnp.bfloat16

HEAD_DIM = 128
MOBA_BLOCK = 256
MOBA_TOPK = 3
TOP_K = 4
SWIGLU_ALPHA = 1.702
SWIGLU_LIMIT = 7.0
RMS_EPS = 1e-5
LOG2E = 1.4426950408889634
LANES = 128
NEG_BIG = -1e30
VMEM_LIMIT = 56 * 1024 * 1024
N_PEN_COLS = 32
AUG_BIAS_COLS = 6
SB_DONE_BITS = 160.0
POS_SPLIT = 64


def _bf16_pieces(x, n):
    out, r = [], np.asarray(x, np.float32)
    for _ in range(n):
        piece = (r.view(np.uint32) & np.uint32(0xFFFF0000)).view(np.float32)
        out.append(piece)
        r = (r - piece).astype(np.float32)
    return out


def _cparams(sem):
    return pltpu.CompilerParams(dimension_semantics=sem, vmem_limit_bytes=VMEM_LIMIT)


def _tile(dim, target, align=LANES):
    if dim <= target:
        return dim
    t = (target // align) * align
    while t > align and dim % t:
        t -= align
    assert dim % t == 0, (dim, target)
    return t


def _rms(x, g):
    ms = jnp.mean(x * x, axis=-1, keepdims=True)
    return x * lax.rsqrt(ms + RMS_EPS) * g


def _rmsnorm_kernel(x_ref, g_ref, o_ref):
    o_ref[...] = _rms(x_ref[...].astype(F32), g_ref[...]).astype(o_ref.dtype)


def rmsnorm(x, g, out_dtype, tm=256):
    n, d = x.shape
    tm = _tile(n, tm, 8)
    return pl.pallas_call(
        _rmsnorm_kernel,
        out_shape=jax.ShapeDtypeStruct((n, d), out_dtype),
        grid=(n // tm,),
        in_specs=[pl.BlockSpec((tm, d), lambda i: (i, 0)),
                  pl.BlockSpec((1, d), lambda i: (0, 0))],
        out_specs=pl.BlockSpec((tm, d), lambda i: (i, 0)),
        compiler_params=_cparams(("parallel",)),
        name="rmsnorm",
    )(x, g.reshape(1, d).astype(F32))


def _mm_kernel(*refs, epilogue):
    if epilogue == "none":
        a_ref, b_ref, o_ref = refs
        e_ref = None
    else:
        a_ref, b_ref, e_ref, o_ref = refs
    acc = jnp.dot(a_ref[...], b_ref[...], preferred_element_type=F32)
    if epilogue == "sigmoid_bias":
        acc = jax.nn.sigmoid(acc + e_ref[...])
    elif epilogue == "residual":
        acc = acc + e_ref[...]
    o_ref[...] = acc.astype(o_ref.dtype)


def matmul(a, b, out_dtype, epilogue="none", extra=None, tm=1024, tn=1024):
    m, kdim = a.shape
    _, n = b.shape
    tm, tn = _tile(m, tm), _tile(n, tn)
    in_specs = [pl.BlockSpec((tm, kdim), lambda i, j: (i, 0)),
                pl.BlockSpec((kdim, tn), lambda i, j: (0, j))]
    args = [a, b]
    if epilogue == "sigmoid_bias":
        in_specs.append(pl.BlockSpec((1, tn), lambda i, j: (0, j)))
        args.append(extra.reshape(1, n).astype(F32))
    elif epilogue == "residual":
        in_specs.append(pl.BlockSpec((tm, tn), lambda i, j: (i, j)))
        args.append(extra)
    return pl.pallas_call(
        functools.partial(_mm_kernel, epilogue=epilogue),
        out_shape=jax.ShapeDtypeStruct((m, n), out_dtype),
        grid=(m // tm, n // tn),
        in_specs=in_specs,
        out_specs=pl.BlockSpec((tm, tn), lambda i, j: (i, j)),
        compiler_params=_cparams(("parallel", "parallel")),
        name="matmul_" + epilogue,
    )(*args)


def _mix_kernel(osb_ref, omb_ref, wsb_ref, wmb_ref, g0_ref, g1_ref, o_ref):
    u_sb = jnp.dot(osb_ref[...], wsb_ref[...], preferred_element_type=F32)
    u_mb = jnp.dot(omb_ref[...], wmb_ref[...], preferred_element_type=F32)
    mixed = g0_ref[...].astype(F32) * u_sb + g1_ref[...].astype(F32) * u_mb
    o_ref[...] = mixed.astype(o_ref.dtype)


def branch_mix(o_sb, o_mb, w_sb, w_mb, gates, tm=512, tn=1024):
    m, kdim = o_sb.shape
    d = w_sb.shape[1]
    tm, tn = _tile(m, tm), _tile(d, tn)
    nj = d // tn
    return pl.pallas_call(
        _mix_kernel,
        out_shape=jax.ShapeDtypeStruct((m, d), BF16),
        grid=(m // tm, nj),
        in_specs=[pl.BlockSpec((tm, kdim), lambda i, j: (i, 0)),
                  pl.BlockSpec((tm, kdim), lambda i, j: (i, 0)),
                  pl.BlockSpec((kdim, tn), lambda i, j: (0, j)),
                  pl.BlockSpec((kdim, tn), lambda i, j: (0, j)),
                  pl.BlockSpec((tm, tn), lambda i, j: (i, j)),
                  pl.BlockSpec((tm, tn), lambda i, j: (i, nj + j))],
        out_specs=pl.BlockSpec((tm, tn), lambda i, j: (i, j)),
        compiler_params=_cparams(("parallel", "parallel")),
        name="branch_mix",
    )(o_sb, o_mb, w_sb, w_mb, gates, gates)


def _softplus2(z):
    bits = lax.bitcast_convert_type(z, jnp.uint32) | jnp.uint32(0x80000000)
    neg_abs = lax.bitcast_convert_type(bits, F32)
    return jnp.maximum(z, 0.0) + jnp.log2(1.0 + jnp.exp2(neg_abs))


def _sb_kernel(q_ref, k_ref, v_ref, u_ref, o_ref, acc_ref, run_ref, *, tq, tk, nblk, scale):
    i = pl.program_id(2)
    n_sub = tq // tk
    q = (q_ref[...].astype(F32) * (scale * LOG2E)).astype(BF16)
    dn = (((1,), (1,)), ((), ()))

    def group(j_hi, n, masked):
        lo = j_hi - (n - 1)
        start = pl.multiple_of(lo * tk, tk)
        kb = k_ref[pl.ds(start, n * tk), :]
        vb = v_ref[pl.ds(start, n * tk), :]
        z = lax.dot_general(q, kb, dn, preferred_element_type=F32)
        sp = _softplus2(z)
        if masked:
            row = lax.broadcasted_iota(jnp.int32, (tq, n * tk), 0) + i * tq
            col = lax.broadcasted_iota(jnp.int32, (tq, n * tk), 1) + lo * tk
            past = col < row
            sp = jnp.where(past, sp, 0.0)
        spb = sp.astype(BF16)
        run = run_ref[...]
        parts = [None] * n
        for b in range(n - 1, -1, -1):
            c = jnp.dot(spb[:, b * tk:(b + 1) * tk], u_ref[...], preferred_element_type=F32)
            later = jnp.concatenate([run] * (tk // LANES), axis=1)
            parts[b] = c + later
            run = run + jnp.broadcast_to(c[:, 0:1], run.shape)
        w = jnp.exp2(z - jnp.concatenate(parts, axis=1))
        if masked:
            w = jnp.where(past, w, 0.0)
        acc_ref[...] += jnp.dot(w.astype(BF16), vb, preferred_element_type=F32)
        run_ref[...] = run

    run_ref[...] = jnp.zeros_like(run_ref)
    acc_ref[...] = jnp.zeros_like(acc_ref)

    def diag_body(g, carry):
        group((i + 1) * n_sub - 1 - g * nblk, nblk, True)
        return carry

    lax.fori_loop(0, n_sub // nblk, diag_body, 0)

    n_groups = (i * n_sub) // nblk

    def cond(state):
        g, min_run = state
        return jnp.logical_and(g < n_groups, min_run < SB_DONE_BITS)

    def body(state):
        g, _ = state
        group(i * n_sub - 1 - g * nblk, nblk, False)
        return g + 1, jnp.min(run_ref[...])

    lax.while_loop(cond, body, (jnp.int32(0), jnp.min(run_ref[...])))
    o_ref[...] = acc_ref[...].astype(o_ref.dtype)


def sb_attention(proj, batch, seq, n_heads, q_col, k_col, v_col, tq=1024, tk=256, nblk=2):
    d = HEAD_DIM
    tq = _tile(seq, tq)
    tk = _tile(tq, tk)
    nq = seq // tq
    assert (tq // tk) % nblk == 0
    scale = 1.0 / math.sqrt(d)
    r = jnp.arange(tk)
    u = (r[:, None] >= r[None, :]).astype(BF16)
    return pl.pallas_call(
        functools.partial(_sb_kernel, tq=tq, tk=tk, nblk=nblk, scale=scale),
        out_shape=jax.ShapeDtypeStruct((batch * seq, n_heads * d), BF16),
        grid=(batch, n_heads, nq),
        in_specs=[pl.BlockSpec((tq, d), lambda b, h, i: (b * nq + i, q_col + h)),
                  pl.BlockSpec((seq, d), lambda b, h, i: (b, k_col + h)),
                  pl.BlockSpec((seq, d), lambda b, h, i: (b, v_col + h)),
                  pl.BlockSpec((tk, tk), lambda b, h, i: (0, 0))],
        out_specs=pl.BlockSpec((tq, d), lambda b, h, i: (b * nq + i, h)),
        scratch_shapes=[pltpu.VMEM((tq, d), F32), pltpu.VMEM((tq, LANES), F32)],
        compiler_params=_cparams(("parallel", "parallel", "arbitrary")),
        name="sb_attention",
    )(proj, proj, proj, u)


def _moba_kernel(q_ref, k_ref, v_ref, aug_ref, cst_ref, pool_ref, o_ref,
                 kaug_ref, vaug_ref, kmean_ref, acc_ref, m_ref, s0_ref, s1_ref, *, blk, tq, tkc, scale):
    i = pl.program_id(2)
    d = HEAD_DIM

    @pl.when(i == 0)
    def _():
        kaug_ref[:, :d] = k_ref[...]
        kaug_ref[:, d:] = aug_ref[...]
        vaug_ref[:, :d] = v_ref[...]
        vaug_ref[:, d:] = jnp.ones((vaug_ref.shape[0], d), BF16)
        kmean_ref[...] = jnp.dot(pool_ref[...], k_ref[...], preferred_element_type=F32)

    q_raw = q_ref[...]
    dn = (((1,), (1,)), ((), ()))

    km = kmean_ref[...]
    km_hi = km.astype(BF16)
    km_lo = (km - km_hi.astype(F32)).astype(BF16)
    gate = (lax.dot_general(q_raw, km_hi, dn, preferred_element_type=F32)
            + lax.dot_general(q_raw, km_lo, dn, preferred_element_type=F32))
    lane = lax.broadcasted_iota(jnp.int32, (tq, LANES), 1)
    row_blk = lax.shift_right_logical(lax.broadcasted_iota(jnp.int32, (tq, LANES), 0), int(math.log2(blk)))
    own = i * (tq // blk) + row_blk
    is_past = lane < own
    gate = jnp.where(is_past, gate, -jnp.inf)
    pen = jnp.where(is_past, NEG_BIG, 0.0)
    for r in range(MOBA_TOPK):
        best = jnp.max(gate, axis=-1, keepdims=True)
        first = jnp.min(jnp.where(gate == best, lane, LANES), axis=-1, keepdims=True)
        hit = (lane == first) & is_past
        pen = jnp.where(hit, 0.0, pen)
        gate = jnp.where(lane == first, -jnp.inf, gate)

    qs = (q_raw.astype(F32) * (scale * LOG2E)).astype(BF16)
    aux = jnp.where(lane < N_PEN_COLS, pen, cst_ref[0])
    q_aug = jnp.concatenate([qs, aux.astype(BF16)], axis=1)

    def logits(c, s_ref, diagonal):
        start = pl.multiple_of(c * tkc, tkc)
        kb = kaug_ref[pl.ds(start, tkc), :]
        s = lax.dot_general(q_aug, kb, dn, preferred_element_type=F32)
        if diagonal:
            qpos = lax.broadcasted_iota(jnp.int32, (tq, tkc), 0) + i * tq
            kpos = lax.broadcasted_iota(jnp.int32, (tq, tkc), 1) + c * tkc
            s = jnp.where(kpos <= qpos, s, NEG_BIG)
        s_ref[...] = s

    def softmax_pv(c, s_ref):
        start = pl.multiple_of(c * tkc, tkc)
        vb = vaug_ref[pl.ds(start, tkc), :]
        s = s_ref[...]
        m_old = m_ref[...]
        m_new = jnp.maximum(m_old, jnp.max(s, axis=-1, keepdims=True))
        alpha = jnp.exp2(m_old - m_new)
        p = jnp.exp2(s - m_new)
        acc_ref[...] = alpha * acc_ref[...] + jnp.dot(p.astype(BF16), vb, preferred_element_type=F32)
        m_ref[...] = m_new

    m_ref[...] = jnp.full(m_ref.shape, NEG_BIG, F32)
    acc_ref[...] = jnp.zeros_like(acc_ref)
    c_own = (i * tq) // tkc
    logits(c_own, s0_ref, True)
    pairs = c_own // 2

    def body(g, carry):
        c = c_own - 2 * g
        logits(c - 1, s1_ref, False)
        softmax_pv(c, s0_ref)
        logits(c - 2, s0_ref, False)
        softmax_pv(c - 1, s1_ref)
        return carry

    lax.fori_loop(0, pairs, body, 0)
    c_tail = c_own - 2 * pairs

    @pl.when(c_tail == 1)
    def _():
        logits(0, s1_ref, False)
        softmax_pv(1, s0_ref)
        softmax_pv(0, s1_ref)

    @pl.when(c_tail == 0)
    def _():
        softmax_pv(0, s0_ref)

    acc = acc_ref[...]
    o_ref[...] = (acc[:, :d] / acc[:, d:]).astype(o_ref.dtype)


def moba_attention(proj, batch, seq, n_heads, q_col, k_col, v_col, tq=1024, tkc=1024):
    d = HEAD_DIM
    blk = MOBA_BLOCK
    nb = seq // blk
    assert seq % blk == 0 and nb <= N_PEN_COLS
    tkc = _tile(seq, tkc, blk)
    tq = _tile(tkc, tq, blk)
    nq = seq // tq
    scale = 1.0 / math.sqrt(d)
    slopes = np.exp2(-8.0 * np.arange(1, n_heads + 1, dtype=np.float32) / n_heads).astype(np.float32)
    pieces = _bf16_pieces(slopes * np.float32(LOG2E), 3)
    cst = np.zeros((n_heads, 1, LANES), np.float32)
    for j in range(AUG_BIAS_COLS):
        cst[:, 0, N_PEN_COLS + j] = pieces[j % 3]
    pos = jnp.arange(seq, dtype=jnp.int32)
    onehot = ((pos[:, None] // blk) == jnp.arange(N_PEN_COLS)[None, :]).astype(BF16)
    pos_hi = ((pos // POS_SPLIT) * POS_SPLIT).astype(BF16)
    pos_lo = (pos % POS_SPLIT).astype(BF16)
    aug = jnp.concatenate([
        onehot,
        jnp.stack([pos_hi] * 3 + [pos_lo] * 3, axis=-1),
        jnp.zeros((seq, d - N_PEN_COLS - AUG_BIAS_COLS), BF16)], axis=-1)
    pool = (((jnp.arange(seq)[None, :] // blk) == jnp.arange(LANES)[:, None]).astype(F32) * (1.0 / blk)).astype(BF16)
    return pl.pallas_call(
        functools.partial(_moba_kernel, blk=blk, tq=tq, tkc=tkc, scale=scale),
        out_shape=jax.ShapeDtypeStruct((batch * seq, n_heads * d), BF16),
        grid=(batch, n_heads, nq),
        in_specs=[pl.BlockSpec((tq, d), lambda b, h, i: (b * nq + i, q_col + h)),
                  pl.BlockSpec((seq, d), lambda b, h, i: (b, k_col + h)),
                  pl.BlockSpec((seq, d), lambda b, h, i: (b, v_col + h)),
                  pl.BlockSpec((seq, d), lambda b, h, i: (0, 0)),
                  pl.BlockSpec((1, 1, LANES), lambda b, h, i: (h, 0, 0)),
                  pl.BlockSpec((LANES, seq), lambda b, h, i: (0, 0))],
        out_specs=pl.BlockSpec((tq, d), lambda b, h, i: (b * nq + i, h)),
        scratch_shapes=[pltpu.VMEM((seq, 2 * d), BF16), pltpu.VMEM((seq, 2 * d), BF16),
                        pltpu.VMEM((LANES, d), F32), pltpu.VMEM((tq, 2 * d), F32),
                        pltpu.VMEM((tq, 1), F32), pltpu.VMEM((tq, tkc), F32), pltpu.VMEM((tq, tkc), F32)],
        compiler_params=_cparams(("parallel", "parallel", "arbitrary")),
        name="moba_attention",
    )(proj, proj, proj, aug, jnp.asarray(cst), pool)


def _split(x):
    hi = x.astype(BF16)
    lo = (x - hi.astype(F32)).astype(BF16)
    return hi, lo


def _router_kernel(x_ref, g_ref, w_ref, b_ref, idx_ref, wgt_ref, h3_ref, *, n_experts):
    h = _rms(x_ref[...], g_ref[...])
    h3_ref[...] = h.reshape(h3_ref.shape).astype(h3_ref.dtype)
    h_hi, h_lo = _split(h)
    w_hi, w_lo = _split(w_ref[...])
    logits = (jnp.dot(h_hi, w_hi, preferred_element_type=F32)
              + jnp.dot(h_lo, w_hi, preferred_element_type=F32)
              + jnp.dot(h_hi, w_lo, preferred_element_type=F32)) + b_ref[...]
    tm = logits.shape[0]
    lane = lax.broadcasted_iota(jnp.int32, (tm, LANES), 1)
    logits = jnp.where(lane < n_experts, logits, -jnp.inf)
    idx_out = jnp.zeros((tm, LANES), jnp.int32)
    val_out = jnp.full((tm, LANES), -jnp.inf, F32)
    top = None
    for r in range(TOP_K):
        best = jnp.max(logits, axis=-1, keepdims=True)
        first = jnp.min(jnp.where(logits == best, lane, LANES), axis=-1, keepdims=True)
        if r == 0:
            top = best
        idx_out = jnp.where(lane == r, first, idx_out)
        val_out = jnp.where(lane == r, best, val_out)
        logits = jnp.where(lane == first, -jnp.inf, logits)
    e = jnp.exp(val_out - top)
    idx_ref[...] = idx_out
    wgt_ref[...] = e / jnp.sum(e, axis=-1, keepdims=True)


def router(x, g, w_router, b_router, tm=256):
    n, d = x.shape
    n_experts = w_router.shape[1]
    tm = _tile(n, tm, 8)
    w_pad = jnp.zeros((d, LANES), F32).at[:, :n_experts].set(w_router)
    b_pad = jnp.zeros((1, LANES), F32).at[0, :n_experts].set(b_router)
    idx, wgt, h3 = pl.pallas_call(
        functools.partial(_router_kernel, n_experts=n_experts),
        out_shape=(jax.ShapeDtypeStruct((n, LANES), jnp.int32),
                   jax.ShapeDtypeStruct((n, LANES), F32),
                   jax.ShapeDtypeStruct((n, d // LANES, LANES), BF16)),
        grid=(n // tm,),
        in_specs=[pl.BlockSpec((tm, d), lambda i: (i, 0)),
                  pl.BlockSpec((1, d), lambda i: (0, 0)),
                  pl.BlockSpec((d, LANES), lambda i: (0, 0)),
                  pl.BlockSpec((1, LANES), lambda i: (0, 0))],
        out_specs=(pl.BlockSpec((tm, LANES), lambda i: (i, 0)),
                   pl.BlockSpec((tm, LANES), lambda i: (i, 0)),
                   pl.BlockSpec((tm, d // LANES, LANES), lambda i: (i, 0, 0))),
        compiler_params=_cparams(("parallel",)),
        name="router",
    )(x, g.reshape(1, d).astype(F32), w_pad, b_pad)
    return idx[:, :TOP_K], wgt, h3


def _row_copy(src_hbm, dst_vmem, sem, src_row, dst_row):
    return pltpu.make_async_copy(src_hbm.at[pl.ds(src_row, 1)], dst_vmem.at[pl.ds(dst_row, 1)], sem)


def _gather_kernel(cur_ref, nxt_ref, h_hbm, o_ref, buf_ref, sem, *, rows, n_steps):
    s = pl.program_id(0)

    def start_rows(idx_ref, slot):
        def start(r, c):
            _row_copy(h_hbm, buf_ref.at[slot], sem.at[slot], idx_ref[0, 0, r], r).start()
            return c
        lax.fori_loop(0, rows, start, 0, unroll=8)

    def wait_rows(slot):
        def wait(r, c):
            _row_copy(h_hbm, buf_ref.at[slot], sem.at[slot], 0, r).wait()
            return c
        lax.fori_loop(0, rows, wait, 0, unroll=8)

    @pl.when(s == 0)
    def _():
        start_rows(cur_ref, 0)

    for slot in range(2):
        @pl.when(lax.rem(s, 2) == slot)
        def _():
            @pl.when(s + 1 < n_steps)
            def _():
                start_rows(nxt_ref, 1 - slot)
            wait_rows(slot)
            o_ref[...] = buf_ref[slot].astype(F32).reshape(o_ref.shape).astype(o_ref.dtype)


def gather_rows(h3, src, rows=512):
    n, dc, lanes = h3.shape
    d = dc * lanes
    p = src.shape[0]
    rows = _tile(p, rows, 8)
    nt = p // rows
    src3 = src.reshape(nt, 1, rows)
    return pl.pallas_call(
        functools.partial(_gather_kernel, rows=rows, n_steps=nt),
        out_shape=jax.ShapeDtypeStruct((p, d), BF16),
        grid=(nt,),
        in_specs=[pl.BlockSpec((1, 1, rows), lambda i: (i, 0, 0), memory_space=pltpu.SMEM),
                  pl.BlockSpec((1, 1, rows), lambda i: (jnp.minimum(i + 1, nt - 1), 0, 0), memory_space=pltpu.SMEM),
                  pl.BlockSpec(memory_space=pl.ANY)],
        out_specs=pl.BlockSpec((rows, d), lambda i: (i, 0)),
        scratch_shapes=[pltpu.VMEM((2, rows, dc, lanes), BF16), pltpu.SemaphoreType.DMA((2,))],
        compiler_params=_cparams(("arbitrary",)),
        name="moe_gather",
    )(src3, src3, h3)


def _new_expert(te_ref, t):
    return jnp.logical_or(t == 0, te_ref[t] != te_ref[jnp.maximum(t - 1, 0)])


def _gate_up_kernel(te_ref, nv_ref, x_ref, wg_ref, wu_ref, bg_ref, bu_ref, o_ref, wgb_ref, wub_ref):
    t = pl.program_id(1)

    @pl.when(t < nv_ref[0])
    def _():
        @pl.when(_new_expert(te_ref, t))
        def _():
            wgb_ref[...] = wg_ref[0].astype(BF16)
            wub_ref[...] = wu_ref[0].astype(BF16)

        x = x_ref[...]
        g = jnp.dot(x, wgb_ref[...], preferred_element_type=F32) + bg_ref[0]
        u = jnp.dot(x, wub_ref[...], preferred_element_type=F32) + bu_ref[0]
        g = jnp.minimum(g, SWIGLU_LIMIT)
        u = jnp.clip(u, -SWIGLU_LIMIT, SWIGLU_LIMIT)
        o_ref[...] = (g * jax.nn.sigmoid(SWIGLU_ALPHA * g) * (u + 1.0)).astype(o_ref.dtype)

    @pl.when(t >= nv_ref[0])
    def _():
        o_ref[...] = jnp.zeros_like(o_ref)


def expert_gate_up(xs, tile_expert, n_valid, w_gate_up, b_gate_up, tm, tf=512):
    p, d = xs.shape
    n_e, _, f2 = w_gate_up.shape
    f = f2 // 2
    tf = _tile(f, tf)
    nf = f // tf
    nt = p // tm
    b3 = b_gate_up.reshape(n_e, 1, f2)

    def row_tile(c, t, te, nv):
        return jnp.minimum(t, nv[0] - 1)

    grid_spec = pltpu.PrefetchScalarGridSpec(
        num_scalar_prefetch=2,
        grid=(nf, nt),
        in_specs=[pl.BlockSpec((tm, d), lambda c, t, te, nv: (row_tile(c, t, te, nv), 0)),
                  pl.BlockSpec((1, d, tf), lambda c, t, te, nv: (te[t], 0, c)),
                  pl.BlockSpec((1, d, tf), lambda c, t, te, nv: (te[t], 0, nf + c)),
                  pl.BlockSpec((1, 1, tf), lambda c, t, te, nv: (te[t], 0, c)),
                  pl.BlockSpec((1, 1, tf), lambda c, t, te, nv: (te[t], 0, nf + c))],
        out_specs=pl.BlockSpec((tm, tf), lambda c, t, te, nv: (t, c)),
        scratch_shapes=[pltpu.VMEM((d, tf), BF16), pltpu.VMEM((d, tf), BF16)],
    )
    return pl.pallas_call(
        _gate_up_kernel,
        out_shape=jax.ShapeDtypeStruct((p, f), BF16),
        grid_spec=grid_spec,
        compiler_params=_cparams(("arbitrary", "arbitrary")),
        name="moe_gate_up",
    )(tile_expert, n_valid, xs, w_gate_up, w_gate_up, b3, b3)


def _down_kernel(te_ref, nv_ref, h_ref, w_ref, b_ref, o_ref, wb_ref):
    t = pl.program_id(1)

    @pl.when(t < nv_ref[0])
    def _():
        @pl.when(_new_expert(te_ref, t))
        def _():
            wb_ref[...] = w_ref[0].astype(BF16)

        y = jnp.dot(h_ref[...], wb_ref[...], preferred_element_type=F32) + b_ref[0]
        o_ref[...] = y.astype(o_ref.dtype)

    @pl.when(t >= nv_ref[0])
    def _():
        o_ref[...] = jnp.zeros_like(o_ref)


def expert_down(hid, tile_expert, n_valid, w_down, b_down, tm, tn=2048):
    p, f = hid.shape
    n_e, _, d = w_down.shape
    tn = _tile(d, tn)
    nd = d // tn
    nt = p // tm
    b3 = b_down.reshape(n_e, 1, d)

    def row_tile(c, t, te, nv):
        return jnp.minimum(t, nv[0] - 1)

    grid_spec = pltpu.PrefetchScalarGridSpec(
        num_scalar_prefetch=2,
        grid=(nd, nt),
        in_specs=[pl.BlockSpec((tm, f), lambda c, t, te, nv: (row_tile(c, t, te, nv), 0)),
                  pl.BlockSpec((1, f, tn), lambda c, t, te, nv: (te[t], 0, c)),
                  pl.BlockSpec((1, 1, tn), lambda c, t, te, nv: (te[t], 0, c))],
        out_specs=pl.BlockSpec((tm, tn), lambda c, t, te, nv: (t, c)),
        scratch_shapes=[pltpu.VMEM((f, tn), BF16)],
    )
    return pl.pallas_call(
        _down_kernel,
        out_shape=jax.ShapeDtypeStruct((p, d), F32),
        grid_spec=grid_spec,
        compiler_params=_cparams(("arbitrary", "arbitrary")),
        name="moe_down",
    )(tile_expert, n_valid, hid, w_down, b3)


def _row_copy2(src_hbm, dst_vmem, sem, src_row, dst_row):
    return pltpu.make_async_copy(src_hbm.at[pl.ds(src_row, 1), :],
                                 dst_vmem.at[pl.ds(dst_row, 1), :], sem)


def _combine_kernel(cur_ref, nxt_ref, x_ref, w_ref, y_hbm, g_ref, o_ref, buf_ref, sem, *, tt, n_steps):
    s = pl.program_id(0)

    def start_rows(idx_ref, slot):
        def start(r, c):
            for k in range(TOP_K):
                _row_copy2(y_hbm, buf_ref.at[slot, k], sem.at[slot], idx_ref[0, 0, r * TOP_K + k], r).start()
            return c
        lax.fori_loop(0, tt, start, 0, unroll=2)

    def wait_rows(slot):
        def wait(r, c):
            for k in range(TOP_K):
                _row_copy2(y_hbm, buf_ref.at[slot, k], sem.at[slot], 0, r).wait()
            return c
        lax.fori_loop(0, tt, wait, 0, unroll=2)

    @pl.when(s == 0)
    def _():
        start_rows(cur_ref, 0)

    for slot in range(2):
        @pl.when(lax.rem(s, 2) == slot)
        def _():
            @pl.when(s + 1 < n_steps)
            def _():
                start_rows(nxt_ref, 1 - slot)
            wait_rows(slot)
            acc = x_ref[...]
            w = w_ref[...]
            for k in range(TOP_K):
                acc = acc + w[:, k:k + 1] * buf_ref[slot, k]
            o_ref[...] = _rms(acc, g_ref[...]).astype(o_ref.dtype)


def combine(x, y, pos, top_w, g, tt=128):
    n, d = x.shape
    tt = _tile(n, tt, 8)
    nt = n // tt
    pos3 = pos.reshape(nt, 1, tt * TOP_K)
    return pl.pallas_call(
        functools.partial(_combine_kernel, tt=tt, n_steps=nt),
        out_shape=jax.ShapeDtypeStruct((n, d), F32),
        grid=(nt,),
        in_specs=[pl.BlockSpec((1, 1, tt * TOP_K), lambda i: (i, 0, 0), memory_space=pltpu.SMEM),
                  pl.BlockSpec((1, 1, tt * TOP_K), lambda i: (jnp.minimum(i + 1, nt - 1), 0, 0),
                               memory_space=pltpu.SMEM),
                  pl.BlockSpec((tt, d), lambda i: (i, 0)),
                  pl.BlockSpec((tt, LANES), lambda i: (i, 0)),
                  pl.BlockSpec(memory_space=pl.ANY),
                  pl.BlockSpec((1, d), lambda i: (0, 0))],
        out_specs=pl.BlockSpec((tt, d), lambda i: (i, 0)),
        scratch_shapes=[pltpu.VMEM((2, TOP_K, tt, d), F32), pltpu.SemaphoreType.DMA((2,))],
        compiler_params=_cparams(("arbitrary",)),
        name="moe_combine",
    )(pos3, pos3, x, top_w, y, g.reshape(1, d).astype(F32))


def route_plan(top_idx, n_experts, tm):
    n = top_idx.shape[0]
    pairs = n * TOP_K
    p_rows = pairs + n_experts * tm
    e_flat = top_idx.reshape(pairs)
    onehot = (e_flat[:, None] == jnp.arange(n_experts)[None, :]).astype(jnp.int32)
    csum = jnp.cumsum(onehot, axis=0)
    rank = jnp.take_along_axis(csum, e_flat[:, None], axis=1)[:, 0] - 1
    counts = csum[-1]
    padded = ((counts + tm - 1) // tm) * tm
    ends = jnp.cumsum(padded)
    starts = ends - padded
    pos = starts[e_flat] + rank
    token = jnp.arange(pairs, dtype=jnp.int32) // TOP_K
    src = jnp.zeros((p_rows,), jnp.int32).at[pos].set(token)
    n_tiles = p_rows // tm
    tile_start = jnp.arange(n_tiles, dtype=jnp.int32) * tm
    tile_expert = jnp.minimum(jnp.sum((ends[None, :] <= tile_start[:, None]).astype(jnp.int32), axis=1), n_experts - 1)
    n_valid = (ends[-1] // tm).astype(jnp.int32).reshape(1)
    tile_expert = jnp.where(tile_start < ends[-1], tile_expert, tile_expert[jnp.maximum(n_valid[0] - 1, 0)])
    return pos.astype(jnp.int32), src, tile_expert.astype(jnp.int32), n_valid


def moe_block(x1, norm_ffn, w_router, b_router, w_gate_up, b_gate_up, w_down, b_down, norm_final, tm=512):
    n_experts = w_router.shape[1]
    top_idx, top_w, h3 = router(x1, norm_ffn, w_router, b_router)
    pos, src, tile_expert, n_valid = route_plan(top_idx, n_experts, tm)
    xs = gather_rows(h3, src)
    hid = expert_gate_up(xs, tile_expert, n_valid, w_gate_up, b_gate_up, tm)
    y = expert_down(hid, tile_expert, n_valid, w_down, b_down, tm)
    return combine(x1, y, pos, top_w, norm_final)


def kernel(x, norm_mix, w_in, w_up_sb, w_up_moba, w_branch_gate, b_branch_gate, w_out, norm_ffn,
           w_router, b_router, w_gate_up, b_gate_up, w_down, b_down, norm_final):
    b, t, d = x.shape
    n = b * t
    assert norm_mix.shape[0] == 1, "single-layer block"
    x2 = x.reshape(n, d)
    c_sb = w_up_sb.shape[1] // HEAD_DIM
    c_mb = w_up_moba.shape[1] // HEAD_DIM
    h = rmsnorm(x2, norm_mix[0], BF16)
    proj = matmul(h, w_in[0].astype(BF16), BF16)
    gates = matmul(h, w_branch_gate[0].astype(BF16), BF16, "sigmoid_bias", b_branch_gate[0])
    o_sb = sb_attention(proj, b, t, c_sb, 0, c_sb, 2 * c_sb)
    o_mb = moba_attention(proj, b, t, c_mb, 3 * c_sb, 3 * c_sb + c_mb, 3 * c_sb + 2 * c_mb)
    mixed = branch_mix(o_sb, o_mb, w_up_sb[0].astype(BF16), w_up_moba[0].astype(BF16), gates)
    x1 = matmul(mixed, w_out[0].astype(BF16), F32, "residual", x2, tm=512)
    out = moe_block(x1, norm_ffn[0], w_router[0], b_router[0], w_gate_up[0], b_gate_up[0],
                    w_down[0], b_down[0], norm_final)
    return out.reshape(b, t, d)
```

```python
import functools
import math

import jax
import jax.numpy as jnp
import numpy as np
from jax import lax
from jax.experimental import pallas as pl
from jax.experimental.pallas import tpu as pltpu

F32 = jnp.float32
BF16 = jnp.bfloat16

HEAD_DIM = 128
MOBA_BLOCK = 256
MOBA_TOPK = 3
TOP_K = 4
SWIGLU_ALPHA = 1.702
SWIGLU_LIMIT = 7.0
RMS_EPS = 1e-5
LOG2E = 1.4426950408889634
LANES = 128
NEG_BIG = -1e30
VMEM_LIMIT = 56 * 1024 * 1024
N_PEN_COLS = 32
AUG_BIAS_COLS = 6
SB_DONE_BITS = 160.0
POS_SPLIT = 64


def _bf16_pieces(x, n):
    out, r = [], np.asarray(x, np.float32)
    for _ in range(n):
        piece = (r.view(np.uint32) & np.uint32(0xFFFF0000)).view(np.float32)
        out.append(piece)
        r = (r - piece).astype(np.float32)
    return out


def _cparams(sem):
    return pltpu.CompilerParams(dimension_semantics=sem, vmem_limit_bytes=VMEM_LIMIT)


def _tile(dim, target, align=LANES):
    if dim <= target:
        return dim
    t = (target // align) * align
    while t > align and dim % t:
        t -= align
    assert dim % t == 0, (dim, target)
    return t


def _rms(x, g):
    ms = jnp.mean(x * x, axis=-1, keepdims=True)
    return x * lax.rsqrt(ms + RMS_EPS) * g


def _rmsnorm_kernel(x_ref, g_ref, o_ref):
    o_ref[...] = _rms(x_ref[...].astype(F32), g_ref[...]).astype(o_ref.dtype)


def rmsnorm(x, g, out_dtype, tm=256):
    n, d = x.shape
    tm = _tile(n, tm, 8)
    return pl.pallas_call(
        _rmsnorm_kernel,
        out_shape=jax.ShapeDtypeStruct((n, d), out_dtype),
        grid=(n // tm,),
        in_specs=[pl.BlockSpec((tm, d), lambda i: (i, 0)),
                  pl.BlockSpec((1, d), lambda i: (0, 0))],
        out_specs=pl.BlockSpec((tm, d), lambda i: (i, 0)),
        compiler_params=_cparams(("parallel",)),
        name="rmsnorm",
    )(x, g.reshape(1, d).astype(F32))


def _mm_kernel(*refs, epilogue):
    if epilogue == "none":
        a_ref, b_ref, o_ref = refs
        e_ref = None
    else:
        a_ref, b_ref, e_ref, o_ref = refs
    acc = jnp.dot(a_ref[...], b_ref[...], preferred_element_type=F32)
    if epilogue == "sigmoid_bias":
        acc = jax.nn.sigmoid(acc + e_ref[...])
    elif epilogue == "residual":
        acc = acc + e_ref[...]
    o_ref[...] = acc.astype(o_ref.dtype)


def matmul(a, b, out_dtype, epilogue="none", extra=None, tm=1024, tn=1024):
    m, kdim = a.shape
    _, n = b.shape
    tm, tn = _tile(m, tm), _tile(n, tn)
    in_specs = [pl.BlockSpec((tm, kdim), lambda i, j: (i, 0)),
                pl.BlockSpec((kdim, tn), lambda i, j: (0, j))]
    args = [a, b]
    if epilogue == "sigmoid_bias":
        in_specs.append(pl.BlockSpec((1, tn), lambda i, j: (0, j)))
        args.append(extra.reshape(1, n).astype(F32))
    elif epilogue == "residual":
        in_specs.append(pl.BlockSpec((tm, tn), lambda i, j: (i, j)))
        args.append(extra)
    return pl.pallas_call(
        functools.partial(_mm_kernel, epilogue=epilogue),
        out_shape=jax.ShapeDtypeStruct((m, n), out_dtype),
        grid=(m // tm, n // tn),
        in_specs=in_specs,
        out_specs=pl.BlockSpec((tm, tn), lambda i, j: (i, j)),
        compiler_params=_cparams(("parallel", "parallel")),
        name="matmul_" + epilogue,
    )(*args)


def _mix_kernel(osb_ref, omb_ref, wsb_ref, wmb_ref, g0_ref, g1_ref, o_ref):
    u_sb = jnp.dot(osb_ref[...], wsb_ref[...], preferred_element_type=F32)
    u_mb = jnp.dot(omb_ref[...], wmb_ref[...], preferred_element_type=F32)
    mixed = g0_ref[...].astype(F32) * u_sb + g1_ref[...].astype(F32) * u_mb
    o_ref[...] = mixed.astype(o_ref.dtype)


def branch_mix(o_sb, o_mb, w_sb, w_mb, gates, tm=512, tn=1024):
    m, kdim = o_sb.shape
    d = w_sb.shape[1]
    tm, tn = _tile(m, tm), _tile(d, tn)
    nj = d // tn
    return pl.pallas_call(
        _mix_kernel,
        out_shape=jax.ShapeDtypeStruct((m, d), BF16),
        grid=(m // tm, nj),
        in_specs=[pl.BlockSpec((tm, kdim), lambda i, j: (i, 0)),
                  pl.BlockSpec((tm, kdim), lambda i, j: (i, 0)),
                  pl.BlockSpec((kdim, tn), lambda i, j: (0, j)),
                  pl.BlockSpec((kdim, tn), lambda i, j: (0, j)),
                  pl.BlockSpec((tm, tn), lambda i, j: (i, j)),
                  pl.BlockSpec((tm, tn), lambda i, j: (i, nj + j))],
        out_specs=pl.BlockSpec((tm, tn), lambda i, j: (i, j)),
        compiler_params=_cparams(("parallel", "parallel")),
        name="branch_mix",
    )(o_sb, o_mb, w_sb, w_mb, gates, gates)


def _softplus2(z):
    bits = lax.bitcast_convert_type(z, jnp.uint32) | jnp.uint32(0x80000000)
    neg_abs = lax.bitcast_convert_type(bits, F32)
    return jnp.maximum(z, 0.0) + jnp.log2(1.0 + jnp.exp2(neg_abs))


def _sb_kernel(q_ref, k_ref, v_ref, u_ref, o_ref, acc_ref, run_ref, *, tq, tk, nblk, scale):
    i = pl.program_id(2)
    n_sub = tq // tk
    q = (q_ref[...].astype(F32) * (scale * LOG2E)).astype(BF16)
    dn = (((1,), (1,)), ((), ()))

    def group(j_hi, n, masked):
        lo = j_hi - (n - 1)
        start = pl.multiple_of(lo * tk, tk)
        kb = k_ref[pl.ds(start, n * tk), :]
        vb = v_ref[pl.ds(start, n * tk), :]
        z = lax.dot_general(q, kb, dn, preferred_element_type=F32)
        sp = _softplus2(z)
        if masked:
            row = lax.broadcasted_iota(jnp.int32, (tq, n * tk), 0) + i * tq
            col = lax.broadcasted_iota(jnp.int32, (tq, n * tk), 1) + lo * tk
            past = col < row
            sp = jnp.where(past, sp, 0.0)
        spb = sp.astype(BF16)
        run = run_ref[...]
        parts = [None] * n
        for b in range(n - 1, -1, -1):
            c = jnp.dot(spb[:, b * tk:(b + 1) * tk], u_ref[...], preferred_element_type=F32)
            later = jnp.concatenate([run] * (tk // LANES), axis=1)
            parts[b] = c + later
            run = run + jnp.broadcast_to(c[:, 0:1], run.shape)
        w = jnp.exp2(z - jnp.concatenate(parts, axis=1))
        if masked:
            w = jnp.where(past, w, 0.0)
        acc_ref[...] += jnp.dot(w.astype(BF16), vb, preferred_element_type=F32)
        run_ref[...] = run

    run_ref[...] = jnp.zeros_like(run_ref)
    acc_ref[...] = jnp.zeros_like(acc_ref)

    def diag_body(g, carry):
        group((i + 1) * n_sub - 1 - g * nblk, nblk, True)
        return carry

    lax.fori_loop(0, n_sub // nblk, diag_body, 0)

    n_groups = (i * n_sub) // nblk

    def cond(state):
        g, min_run = state
        return jnp.logical_and(g < n_groups, min_run < SB_DONE_BITS)

    def body(state):
        g, _ = state
        group(i * n_sub - 1 - g * nblk, nblk, False)
        return g + 1, jnp.min(run_ref[...])

    lax.while_loop(cond, body, (jnp.int32(0), jnp.min(run_ref[...])))
    o_ref[...] = acc_ref[...].astype(o_ref.dtype)


def sb_attention(proj, batch, seq, n_heads, q_col, k_col, v_col, tq=1024, tk=256, nblk=2):
    d = HEAD_DIM
    tq = _tile(seq, tq)
    tk = _tile(tq, tk)
    nq = seq // tq
    assert (tq // tk) % nblk == 0
    scale = 1.0 / math.sqrt(d)
    r = jnp.arange(tk)
    u = (r[:, None] >= r[None, :]).astype(BF16)
    return pl.pallas_call(
        functools.partial(_sb_kernel, tq=tq, tk=tk, nblk=nblk, scale=scale),
        out_shape=jax.ShapeDtypeStruct((batch * seq, n_heads * d), BF16),
        grid=(batch, n_heads, nq),
        in_specs=[pl.BlockSpec((tq, d), lambda b, h, i: (b * nq + i, q_col + h)),
                  pl.BlockSpec((seq, d), lambda b, h, i: (b, k_col + h)),
                  pl.BlockSpec((seq, d), lambda b, h, i: (b, v_col + h)),
                  pl.BlockSpec((tk, tk), lambda b, h, i: (0, 0))],
        out_specs=pl.BlockSpec((tq, d), lambda b, h, i: (b * nq + i, h)),
        scratch_shapes=[pltpu.VMEM((tq, d), F32), pltpu.VMEM((tq, LANES), F32)],
        compiler_params=_cparams(("parallel", "parallel", "arbitrary")),
        name="sb_attention",
    )(proj, proj, proj, u)


def _moba_kernel(q_ref, k_ref, v_ref, aug_ref, cst_ref, pool_ref, o_ref,
                 kaug_ref, vaug_ref, kmean_ref, acc_ref, m_ref, s0_ref, s1_ref, *, blk, tq, tkc, scale):
    i = pl.program_id(2)
    d = HEAD_DIM

    @pl.when(i == 0)
    def _():
        kaug_ref[:, :d] = k_ref[...]
        kaug_ref[:, d:] = aug_ref[...]
        vaug_ref[:, :d] = v_ref[...]
        vaug_ref[:, d:] = jnp.ones((vaug_ref.shape[0], d), BF16)
        kmean_ref[...] = jnp.dot(pool_ref[...], k_ref[...], preferred_element_type=F32)

    q_raw = q_ref[...]
    dn = (((1,), (1,)), ((), ()))

    km = kmean_ref[...]
    km_hi = km.astype(BF16)
    km_lo = (km - km_hi.astype(F32)).astype(BF16)
    gate = (lax.dot_general(q_raw, km_hi, dn, preferred_element_type=F32)
            + lax.dot_general(q_raw, km_lo, dn, preferred_element_type=F32))
    lane = lax.broadcasted_iota(jnp.int32, (tq, LANES), 1)
    row_blk = lax.shift_right_logical(lax.broadcasted_iota(jnp.int32, (tq, LANES), 0), int(math.log2(blk)))
    own = i * (tq // blk) + row_blk
    is_past = lane < own
    gate = jnp.where(is_past, gate, -jnp.inf)
    pen = jnp.where(is_past, NEG_BIG, 0.0)
    for r in range(MOBA_TOPK):
        best = jnp.max(gate, axis=-1, keepdims=True)
        first = jnp.min(jnp.where(gate == best, lane, LANES), axis=-1, keepdims=True)
        hit = (lane == first) & is_past
        pen = jnp.where(hit, 0.0, pen)
        gate = jnp.where(lane == first, -jnp.inf, gate)

    qs = (q_raw.astype(F32) * (scale * LOG2E)).astype(BF16)
    aux = jnp.where(lane < N_PEN_COLS, pen, cst_ref[0])
    q_aug = jnp.concatenate([qs, aux.astype(BF16)], axis=1)

    def logits(c, s_ref, diagonal):
        start = pl.multiple_of(c * tkc, tkc)
        kb = kaug_ref[pl.ds(start, tkc), :]
        s = lax.dot_general(q_aug, kb, dn, preferred_element_type=F32)
        if diagonal:
            qpos = lax.broadcasted_iota(jnp.int32, (tq, tkc), 0) + i * tq
            kpos = lax.broadcasted_iota(jnp.int32, (tq, tkc), 1) + c * tkc
            s = jnp.where(kpos <= qpos, s, NEG_BIG)
        s_ref[...] = s

    def softmax_pv(c, s_ref):
        start = pl.multiple_of(c * tkc, tkc)
        vb = vaug_ref[pl.ds(start, tkc), :]
        s = s_ref[...]
        m_old = m_ref[...]
        m_new = jnp.maximum(m_old, jnp.max(s, axis=-1, keepdims=True))
        alpha = jnp.exp2(m_old - m_new)
        p = jnp.exp2(s - m_new)
        acc_ref[...] = alpha * acc_ref[...] + jnp.dot(p.astype(BF16), vb, preferred_element_type=F32)
        m_ref[...] = m_new

    m_ref[...] = jnp.full(m_ref.shape, NEG_BIG, F32)
    acc_ref[...] = jnp.zeros_like(acc_ref)
    c_own = (i * tq) // tkc
    logits(c_own, s0_ref, True)
    pairs = c_own // 2

    def body(g, carry):
        c = c_own - 2 * g
        logits(c - 1, s1_ref, False)
        softmax_pv(c, s0_ref)
        logits(c - 2, s0_ref, False)
        softmax_pv(c - 1, s1_ref)
        return carry

    lax.fori_loop(0, pairs, body, 0)
    c_tail = c_own - 2 * pairs

    @pl.when(c_tail == 1)
    def _():
        logits(0, s1_ref, False)
        softmax_pv(1, s0_ref)
        softmax_pv(0, s1_ref)

    @pl.when(c_tail == 0)
    def _():
        softmax_pv(0, s0_ref)

    acc = acc_ref[...]
    o_ref[...] = (acc[:, :d] / acc[:, d:]).astype(o_ref.dtype)


def moba_attention(proj, batch, seq, n_heads, q_col, k_col, v_col, tq=1024, tkc=1024):
    d = HEAD_DIM
    blk = MOBA_BLOCK
    nb = seq // blk
    assert seq % blk == 0 and nb <= N_PEN_COLS
    tkc = _tile(seq, tkc, blk)
    tq = _tile(tkc, tq, blk)
    nq = seq // tq
    scale = 1.0 / math.sqrt(d)
    slopes = np.exp2(-8.0 * np.arange(1, n_heads + 1, dtype=np.float32) / n_heads).astype(np.float32)
    pieces = _bf16_pieces(slopes * np.float32(LOG2E), 3)
    cst = np.zeros((n_heads, 1, LANES), np.float32)
    for j in range(AUG_BIAS_COLS):
        cst[:, 0, N_PEN_COLS + j] = pieces[j % 3]
    pos = jnp.arange(seq, dtype=jnp.int32)
    onehot = ((pos[:, None] // blk) == jnp.arange(N_PEN_COLS)[None, :]).astype(BF16)
    pos_hi = ((pos // POS_SPLIT) * POS_SPLIT).astype(BF16)
    pos_lo = (pos % POS_SPLIT).astype(BF16)
    aug = jnp.concatenate([
        onehot,
        jnp.stack([pos_hi] * 3 + [pos_lo] * 3, axis=-1),
        jnp.zeros((seq, d - N_PEN_COLS - AUG_BIAS_COLS), BF16)], axis=-1)
    pool = (((jnp.arange(seq)[None, :] // blk) == jnp.arange(LANES)[:, None]).astype(F32) * (1.0 / blk)).astype(BF16)
    return pl.pallas_call(
        functools.partial(_moba_kernel, blk=blk, tq=tq, tkc=tkc, scale=scale),
        out_shape=jax.ShapeDtypeStruct((batch * seq, n_heads * d), BF16),
        grid=(batch, n_heads, nq),
        in_specs=[pl.BlockSpec((tq, d), lambda b, h, i: (b * nq + i, q_col + h)),
                  pl.BlockSpec((seq, d), lambda b, h, i: (b, k_col + h)),
                  pl.BlockSpec((seq, d), lambda b, h, i: (b, v_col + h)),
                  pl.BlockSpec((seq, d), lambda b, h, i: (0, 0)),
                  pl.BlockSpec((1, 1, LANES), lambda b, h, i: (h, 0, 0)),
                  pl.BlockSpec((LANES, seq), lambda b, h, i: (0, 0))],
        out_specs=pl.BlockSpec((tq, d), lambda b, h, i: (b * nq + i, h)),
        scratch_shapes=[pltpu.VMEM((seq, 2 * d), BF16), pltpu.VMEM((seq, 2 * d), BF16),
                        pltpu.VMEM((LANES, d), F32), pltpu.VMEM((tq, 2 * d), F32),
                        pltpu.VMEM((tq, 1), F32), pltpu.VMEM((tq, tkc), F32), pltpu.VMEM((tq, tkc), F32)],
        compiler_params=_cparams(("parallel", "parallel", "arbitrary")),
        name="moba_attention",
    )(proj, proj, proj, aug, jnp.asarray(cst), pool)


def _split(x):
    hi = x.astype(BF16)
    lo = (x - hi.astype(F32)).astype(BF16)
    return hi, lo


def _router_kernel(x_ref, g_ref, w_ref, b_ref, idx_ref, wgt_ref, h3_ref, *, n_experts):
    h = _rms(x_ref[...], g_ref[...])
    h3_ref[...] = h.reshape(h3_ref.shape).astype(h3_ref.dtype)
    h_hi, h_lo = _split(h)
    w_hi, w_lo = _split(w_ref[...])
    logits = (jnp.dot(h_hi, w_hi, preferred_element_type=F32)
              + jnp.dot(h_lo, w_hi, preferred_element_type=F32)
              + jnp.dot(h_hi, w_lo, preferred_element_type=F32)) + b_ref[...]
    tm = logits.shape[0]
    lane = lax.broadcasted_iota(jnp.int32, (tm, LANES), 1)
    logits = jnp.where(lane < n_experts, logits, -jnp.inf)
    idx_out = jnp.zeros((tm, LANES), jnp.int32)
    val_out = jnp.full((tm, LANES), -jnp.inf, F32)
    top = None
    for r in range(TOP_K):
        best = jnp.max(logits, axis=-1, keepdims=True)
        first = jnp.min(jnp.where(logits == best, lane, LANES), axis=-1, keepdims=True)
        if r == 0:
            top = best
        idx_out = jnp.where(lane == r, first, idx_out)
        val_out = jnp.where(lane == r, best, val_out)
        logits = jnp.where(lane == first, -jnp.inf, logits)
    e = jnp.exp(val_out - top)
    idx_ref[...] = idx_out
    wgt_ref[...] = e / jnp.sum(e, axis=-1, keepdims=True)


def router(x, g, w_router, b_router, tm=256):
    n, d = x.shape
    n_experts = w_router.shape[1]
    tm = _tile(n, tm, 8)
    w_pad = jnp.zeros((d, LANES), F32).at[:, :n_experts].set(w_router)
    b_pad = jnp.zeros((1, LANES), F32).at[0, :n_experts].set(b_router)
    idx, wgt, h3 = pl.pallas_call(
        functools.partial(_router_kernel, n_experts=n_experts),
        out_shape=(jax.ShapeDtypeStruct((n, LANES), jnp.int32),
                   jax.ShapeDtypeStruct((n, LANES), F32),
                   jax.ShapeDtypeStruct((n, d // LANES, LANES), F32)),
        grid=(n // tm,),
        in_specs=[pl.BlockSpec((tm, d), lambda i: (i, 0)),
                  pl.BlockSpec((1, d), lambda i: (0, 0)),
                  pl.BlockSpec((d, LANES), lambda i: (0, 0)),
                  pl.BlockSpec((1, LANES), lambda i: (0, 0))],
        out_specs=(pl.BlockSpec((tm, LANES), lambda i: (i, 0)),
                   pl.BlockSpec((tm, LANES), lambda i: (i, 0)),
                   pl.BlockSpec((tm, d // LANES, LANES), lambda i: (i, 0, 0))),
        compiler_params=_cparams(("parallel",)),
        name="router",
    )(x, g.reshape(1, d).astype(F32), w_pad, b_pad)
    return idx[:, :TOP_K], wgt, h3


SUBLANES = 8


def _tile_row_copy(h_hbm, buf_ref, sem, src_row, dst_row):
    return pltpu.make_async_copy(h_hbm.at[src_row],
                                 buf_ref.at[dst_row // SUBLANES, :, dst_row % SUBLANES, :], sem)


def _gather_kernel(cur_ref, nxt_ref, h_hbm, o_ref, buf_ref, sem, *, rows, n_steps):
    s = pl.program_id(0)
    dc = buf_ref.shape[2]

    def start_rows(idx_ref, slot):
        def start(r, c):
            _tile_row_copy(h_hbm, buf_ref.at[slot], sem.at[slot], idx_ref[0, 0, r], r).start()
            return c
        lax.fori_loop(0, rows, start, 0, unroll=8)

    def wait_rows(slot):
        def wait(r, c):
            _tile_row_copy(h_hbm, buf_ref.at[slot], sem.at[slot], 0, r).wait()
            return c
        lax.fori_loop(0, rows, wait, 0, unroll=8)

    @pl.when(s == 0)
    def _():
        start_rows(cur_ref, 0)

    for slot in range(2):
        @pl.when(lax.rem(s, 2) == slot)
        def _():
            @pl.when(s + 1 < n_steps)
            def _():
                start_rows(nxt_ref, 1 - slot)
            wait_rows(slot)
            for j in range(dc):
                o_ref[:, j * LANES:(j + 1) * LANES] = buf_ref[slot, :, j].reshape(rows, LANES).astype(o_ref.dtype)


def gather_rows(h3, src, rows=256):
    n, dc, lanes = h3.shape
    d = dc * lanes
    p = src.shape[0]
    rows = _tile(p, rows, 8)
    nt = p // rows
    src3 = src.reshape(nt, 1, rows)
    return pl.pallas_call(
        functools.partial(_gather_kernel, rows=rows, n_steps=nt),
        out_shape=jax.ShapeDtypeStruct((p, d), BF16),
        grid=(nt,),
        in_specs=[pl.BlockSpec((1, 1, rows), lambda i: (i, 0, 0), memory_space=pltpu.SMEM),
                  pl.BlockSpec((1, 1, rows), lambda i: (jnp.minimum(i + 1, nt - 1), 0, 0), memory_space=pltpu.SMEM),
                  pl.BlockSpec(memory_space=pl.ANY)],
        out_specs=pl.BlockSpec((rows, d), lambda i: (i, 0)),
        scratch_shapes=[pltpu.VMEM((2, rows // SUBLANES, dc, SUBLANES, lanes), F32), pltpu.SemaphoreType.DMA((2,))],
        compiler_params=_cparams(("arbitrary",)),
        name="moe_gather",
    )(src3, src3, h3)


def _new_expert(te_ref, t):
    return jnp.logical_or(t == 0, te_ref[t] != te_ref[jnp.maximum(t - 1, 0)])


def _gate_up_kernel(te_ref, nv_ref, x_ref, wg_ref, wu_ref, bg_ref, bu_ref, o_ref, wgb_ref, wub_ref):
    t = pl.program_id(1)

    @pl.when(t < nv_ref[0])
    def _():
        @pl.when(_new_expert(te_ref, t))
        def _():
            wgb_ref[...] = wg_ref[0].astype(BF16)
            wub_ref[...] = wu_ref[0].astype(BF16)

        x = x_ref[...]
        g = jnp.dot(x, wgb_ref[...], preferred_element_type=F32) + bg_ref[0]
        u = jnp.dot(x, wub_ref[...], preferred_element_type=F32) + bu_ref[0]
        g = jnp.minimum(g, SWIGLU_LIMIT)
        u = jnp.clip(u, -SWIGLU_LIMIT, SWIGLU_LIMIT)
        o_ref[...] = (g * jax.nn.sigmoid(SWIGLU_ALPHA * g) * (u + 1.0)).astype(o_ref.dtype)

    @pl.when(t >= nv_ref[0])
    def _():
        o_ref[...] = jnp.zeros_like(o_ref)


def expert_gate_up(xs, tile_expert, n_valid, w_gate_up, b_gate_up, tm, tf=512):
    p, d = xs.shape
    n_e, _, f2 = w_gate_up.shape
    f = f2 // 2
    tf = _tile(f, tf)
    nf = f // tf
    nt = p // tm
    b3 = b_gate_up.reshape(n_e, 1, f2)

    def row_tile(c, t, te, nv):
        return jnp.minimum(t, nv[0] - 1)

    grid_spec = pltpu.PrefetchScalarGridSpec(
        num_scalar_prefetch=2,
        grid=(nf, nt),
        in_specs=[pl.BlockSpec((tm, d), lambda c, t, te, nv: (row_tile(c, t, te, nv), 0)),
                  pl.BlockSpec((1, d, tf), lambda c, t, te, nv: (te[t], 0, c)),
                  pl.BlockSpec((1, d, tf), lambda c, t, te, nv: (te[t], 0, nf + c)),
                  pl.BlockSpec((1, 1, tf), lambda c, t, te, nv: (te[t], 0, c)),
                  pl.BlockSpec((1, 1, tf), lambda c, t, te, nv: (te[t], 0, nf + c))],
        out_specs=pl.BlockSpec((tm, tf), lambda c, t, te, nv: (t, c)),
        scratch_shapes=[pltpu.VMEM((d, tf), BF16), pltpu.VMEM((d, tf), BF16)],
    )
    return pl.pallas_call(
        _gate_up_kernel,
        out_shape=jax.ShapeDtypeStruct((p, f), BF16),
        grid_spec=grid_spec,
        compiler_params=_cparams(("arbitrary", "arbitrary")),
        name="moe_gate_up",
    )(tile_expert, n_valid, xs, w_gate_up, w_gate_up, b3, b3)


def _down_kernel(te_ref, nv_ref, h_ref, w_ref, b_ref, o_ref, wb_ref):
    t = pl.program_id(1)

    @pl.when(t < nv_ref[0])
    def _():
        @pl.when(_new_expert(te_ref, t))
        def _():
            wb_ref[...] = w_ref[0].astype(BF16)

        y = jnp.dot(h_ref[...], wb_ref[...], preferred_element_type=F32) + b_ref[0]
        o_ref[...] = y.astype(o_ref.dtype)

    @pl.when(t >= nv_ref[0])
    def _():
        o_ref[...] = jnp.zeros_like(o_ref)


def expert_down(hid, tile_expert, n_valid, w_down, b_down, tm, tn=2048):
    p, f = hid.shape
    n_e, _, d = w_down.shape
    tn = _tile(d, tn)
    nd = d // tn
    nt = p // tm
    b3 = b_down.reshape(n_e, 1, d)

    def row_tile(c, t, te, nv):
        return jnp.minimum(t, nv[0] - 1)

    grid_spec = pltpu.PrefetchScalarGridSpec(
        num_scalar_prefetch=2,
        grid=(nd, nt),
        in_specs=[pl.BlockSpec((tm, f), lambda c, t, te, nv: (row_tile(c, t, te, nv), 0)),
                  pl.BlockSpec((1, f, tn), lambda c, t, te, nv: (te[t], 0, c)),
                  pl.BlockSpec((1, 1, tn), lambda c, t, te, nv: (te[t], 0, c))],
        out_specs=pl.BlockSpec((tm, tn), lambda c, t, te, nv: (t, c)),
        scratch_shapes=[pltpu.VMEM((f, tn), BF16)],
    )
    return pl.pallas_call(
        _down_kernel,
        out_shape=jax.ShapeDtypeStruct((p, d), F32),
        grid_spec=grid_spec,
        compiler_params=_cparams(("arbitrary", "arbitrary")),
        name="moe_down",
    )(tile_expert, n_valid, hid, w_down, b3)


def _row_copy2(src_hbm, dst_vmem, sem, src_row, dst_row):
    return pltpu.make_async_copy(src_hbm.at[pl.ds(src_row, 1), :],
                                 dst_vmem.at[pl.ds(dst_row, 1), :], sem)


def _combine_kernel(cur_ref, nxt_ref, x_ref, w_ref, y_hbm, g_ref, o_ref, buf_ref, sem, *, tt, n_steps):
    s = pl.program_id(0)

    def start_rows(idx_ref, slot):
        def start(r, c):
            for k in range(TOP_K):
                _row_copy2(y_hbm, buf_ref.at[slot, k], sem.at[slot], idx_ref[0, 0, r * TOP_K + k], r).start()
            return c
        lax.fori_loop(0, tt, start, 0, unroll=2)

    def wait_rows(slot):
        def wait(r, c):
            for k in range(TOP_K):
                _row_copy2(y_hbm, buf_ref.at[slot, k], sem.at[slot], 0, r).wait()
            return c
        lax.fori_loop(0, tt, wait, 0, unroll=2)

    @pl.when(s == 0)
    def _():
        start_rows(cur_ref, 0)

    for slot in range(2):
        @pl.when(lax.rem(s, 2) == slot)
        def _():
            @pl.when(s + 1 < n_steps)
            def _():
                start_rows(nxt_ref, 1 - slot)
            wait_rows(slot)
            acc = x_ref[...]
            w = w_ref[...]
            for k in range(TOP_K):
                acc = acc + w[:, k:k + 1] * buf_ref[slot, k]
            o_ref[...] = _rms(acc, g_ref[...]).astype(o_ref.dtype)


def combine(x, y, pos, top_w, g, tt=128):
    n, d = x.shape
    tt = _tile(n, tt, 8)
    nt = n // tt
    pos3 = pos.reshape(nt, 1, tt * TOP_K)
    return pl.pallas_call(
        functools.partial(_combine_kernel, tt=tt, n_steps=nt),
        out_shape=jax.ShapeDtypeStruct((n, d), F32),
        grid=(nt,),
        in_specs=[pl.BlockSpec((1, 1, tt * TOP_K), lambda i: (i, 0, 0), memory_space=pltpu.SMEM),
                  pl.BlockSpec((1, 1, tt * TOP_K), lambda i: (jnp.minimum(i + 1, nt - 1), 0, 0),
                               memory_space=pltpu.SMEM),
                  pl.BlockSpec((tt, d), lambda i: (i, 0)),
                  pl.BlockSpec((tt, LANES), lambda i: (i, 0)),
                  pl.BlockSpec(memory_space=pl.ANY),
                  pl.BlockSpec((1, d), lambda i: (0, 0))],
        out_specs=pl.BlockSpec((tt, d), lambda i: (i, 0)),
        scratch_shapes=[pltpu.VMEM((2, TOP_K, tt, d), F32), pltpu.SemaphoreType.DMA((2,))],
        compiler_params=_cparams(("arbitrary",)),
        name="moe_combine",
    )(pos3, pos3, x, top_w, y, g.reshape(1, d).astype(F32))


def route_plan(top_idx, n_experts, tm):
    n = top_idx.shape[0]
    pairs = n * TOP_K
    p_rows = pairs + n_experts * tm
    e_flat = top_idx.reshape(pairs)
    onehot = (e_flat[:, None] == jnp.arange(n_experts)[None, :]).astype(jnp.int32)
    csum = jnp.cumsum(onehot, axis=0)
    rank = jnp.take_along_axis(csum, e_flat[:, None], axis=1)[:, 0] - 1
    counts = csum[-1]
    padded = ((counts + tm - 1) // tm) * tm
    ends = jnp.cumsum(padded)
    starts = ends - padded
    pos = starts[e_flat] + rank
    token = jnp.arange(pairs, dtype=jnp.int32) // TOP_K
    src = jnp.zeros((p_rows,), jnp.int32).at[pos].set(token)
    n_tiles = p_rows // tm
    tile_start = jnp.arange(n_tiles, dtype=jnp.int32) * tm
    tile_expert = jnp.minimum(jnp.sum((ends[None, :] <= tile_start[:, None]).astype(jnp.int32), axis=1), n_experts - 1)
    n_valid = (ends[-1] // tm).astype(jnp.int32).reshape(1)
    tile_expert = jnp.where(tile_start < ends[-1], tile_expert, tile_expert[jnp.maximum(n_valid[0] - 1, 0)])
    return pos.astype(jnp.int32), src, tile_expert.astype(jnp.int32), n_valid


def moe_block(x1, norm_ffn, w_router, b_router, w_gate_up, b_gate_up, w_down, b_down, norm_final, tm=512):
    n_experts = w_router.shape[1]
    top_idx, top_w, h3 = router(x1, norm_ffn, w_router, b_router)
    pos, src, tile_expert, n_valid = route_plan(top_idx, n_experts, tm)
    xs = gather_rows(h3, src)
    hid = expert_gate_up(xs, tile_expert, n_valid, w_gate_up, b_gate_up, tm)
    y = expert_down(hid, tile_expert, n_valid, w_down, b_down, tm)
    return combine(x1, y, pos, top_w, norm_final)


def kernel(x, norm_mix, w_in, w_up_sb, w_up_moba, w_branch_gate, b_branch_gate, w_out, norm_ffn,
           w_router, b_router, w_gate_up, b_gate_up, w_down, b_down, norm_final):
    b, t, d = x.shape
    n = b * t
    assert norm_mix.shape[0] == 1, "single-layer block"
    x2 = x.reshape(n, d)
    c_sb = w_up_sb.shape[1] // HEAD_DIM
    c_mb = w_up_moba.shape[1] // HEAD_DIM
    h = rmsnorm(x2, norm_mix[0], BF16)
    proj = matmul(h, w_in[0].astype(BF16), BF16)
    gates = matmul(h, w_branch_gate[0].astype(BF16), BF16, "sigmoid_bias", b_branch_gate[0])
    o_sb = sb_attention(proj, b, t, c_sb, 0, c_sb, 2 * c_sb)
    o_mb = moba_attention(proj, b, t, c_mb, 3 * c_sb, 3 * c_sb + c_mb, 3 * c_sb + 2 * c_mb)
    mixed = branch_mix(o_sb, o_mb, w_up_sb[0].astype(BF16), w_up_moba[0].astype(BF16), gates)
    x1 = matmul(mixed, w_out[0].astype(BF16), F32, "residual", x2, tm=512)
    out = moe_block(x1, norm_ffn[0], w_router[0], b_router[0], w_gate_up[0], b_gate_up[0],
                    w_down[0], b_down[0], norm_final)
    return out.reshape(b, t, d)
```
